```python
import math
import jax, jax.numpy as jnp
from jax import lax
import numpy as np

D_MODEL = 1024
BATCH = 4
SEQ = 4096
DEPTH = 2

PLE_DIM = 256
D_FF = 2816
HEAD_DIM = 64
SB_HEADS = 8
SWA_HEADS = 8
SWA_KV_HEADS = 2
WINDOW = 128
Q_BLOCK = 128
GDN_K_HEADS = 8
GDN_V_HEADS = 16
GDN_HEAD_DIM = 128
GDN_CONV = 4
GDN_CHUNK = 64
EPS = 1e-6
N_EVEN = (DEPTH + 1) // 2
N_ODD = DEPTH // 2

SB_W = SB_HEADS * HEAD_DIM
SWA_QW = SWA_HEADS * HEAD_DIM
SWA_KVW = SWA_KV_HEADS * HEAD_DIM
ATT_IN = 3 * SB_W + SWA_QW + 2 * SWA_KVW
ATT_OUT = SB_W + SWA_QW
GDN_KW = GDN_K_HEADS * GDN_HEAD_DIM
GDN_VW = GDN_V_HEADS * GDN_HEAD_DIM
GDN_CONV_W = 2 * GDN_KW + GDN_VW
GDN_IN = GDN_CONV_W + GDN_VW + 2 * GDN_V_HEADS

kernel_name = 'hybrid_stickbreak_swa_gdn_macaron'


def rmsnorm(x, g):
    xf = x.astype(jnp.float32)
    y = xf * lax.rsqrt(jnp.mean(xf * xf, axis=-1, keepdims=True) + EPS) * g.astype(jnp.float32)
    return y.astype(x.dtype)


def l2norm(x):
    xf = x.astype(jnp.float32)
    return xf * lax.rsqrt(jnp.sum(xf * xf, axis=-1, keepdims=True) + EPS)


def swiglu(h, w_gate, w_up, w_down):
    return (jax.nn.silu(h @ w_gate) * (h @ w_up)) @ w_down


def alibi_slopes(n):
    return jnp.asarray(2.0 ** (-8.0 * (np.arange(n) + 1) / n), dtype=jnp.float32)


def stick_breaking_attention(q, k, v):
    b, h, t, d = q.shape
    nblk = t // Q_BLOCK
    qb = q.reshape(b, h, nblk, Q_BLOCK, d).transpose(2, 0, 1, 3, 4)
    key_pos = jnp.arange(t)
    scale = d ** -0.5

    def block(args):
        qi, i = args
        z = jnp.einsum('bhqd,bhkd->bhqk', qi, k).astype(jnp.float32) * scale
        q_pos = i * Q_BLOCK + jnp.arange(Q_BLOCK)
        causal = key_pos[None, :] < q_pos[:, None]
        log_keep = jnp.where(causal, jax.nn.log_sigmoid(-z), 0.0)
        log_between = lax.cumsum(log_keep, axis=3, reverse=True) - log_keep
        w = jnp.where(causal, jnp.exp(jax.nn.log_sigmoid(z) + log_between), 0.0)
        return jnp.einsum('bhqk,bhkd->bhqd', w.astype(v.dtype), v)

    out = lax.map(block, (qb, jnp.arange(nblk)))
    return out.transpose(1, 2, 0, 3, 4).reshape(b, h, t, d)


def swa_sink_attention(q, k, v, q_gain, k_gain, sinks, slopes):
    q = rmsnorm(q, q_gain)
    k = rmsnorm(k, k_gain)
    b, t, hq, d = q.shape
    hkv = k.shape[2]
    g = hq // hkv
    nblk = t // WINDOW
    qb = q.reshape(b, nblk, WINDOW, hkv, g, d)
    kpad = jnp.pad(k, ((0, 0), (WINDOW, 0), (0, 0), (0, 0)))
    vpad = jnp.pad(v, ((0, 0), (WINDOW, 0), (0, 0), (0, 0)))
    kb = jnp.concatenate([kpad[:, :t].reshape(b, nblk, WINDOW, hkv, d),
                          k.reshape(b, nblk, WINDOW, hkv, d)], axis=2)
    vb = jnp.concatenate([vpad[:, :t].reshape(b, nblk, WINDOW, hkv, d),
                          v.reshape(b, nblk, WINDOW, hkv, d)], axis=2)
    s = jnp.einsum('bnqhgd,bnkhd->bnhgqk', qb, kb).astype(jnp.float32) * (d ** -0.5)
    qi = jnp.arange(WINDOW)[:, None]
    kj = jnp.arange(2 * WINDOW)[None, :]
    dist = (qi + WINDOW - kj)
    band = (dist >= 0) & (dist < WINDOW)
    valid = band[None] & ((jnp.arange(nblk)[:, None, None] > 0) | (kj >= WINDOW)[None])
    bias = -slopes.reshape(hkv, g)[:, :, None, None] * dist.astype(jnp.float32)
    s = jnp.where(valid[None, :, None, None], s + bias[None, None], -jnp.inf)
    sink = jnp.broadcast_to(sinks.astype(jnp.float32).reshape(hkv, g)[None, None, :, :, None, None],
                            s.shape[:-1] + (1,))
    probs = jax.nn.softmax(jnp.concatenate([s, sink], axis=-1), axis=-1)[..., :-1]
    o = jnp.einsum('bnhgqk,bnkhd->bnqhgd', probs.astype(v.dtype), vb)
    return o.reshape(b, t, hq, d)


def attention_mixer(h, w_in, q_gain, k_gain, sinks, w_out):
    b, t, _ = h.shape
    proj = h @ w_in
    cuts = [SB_W, 2 * SB_W, 3 * SB_W, 3 * SB_W + SWA_QW, 3 * SB_W + SWA_QW + SWA_KVW]
    sq, sk, sv, bq, bk, bv = jnp.split(proj, cuts, axis=-1)
    heads = lambda z, n: z.reshape(b, t, n, HEAD_DIM)
    a_out = stick_breaking_attention(heads(sq, SB_HEADS).transpose(0, 2, 1, 3),
                                     heads(sk, SB_HEADS).transpose(0, 2, 1, 3),
                                     heads(sv, SB_HEADS).transpose(0, 2, 1, 3))
    b_out = swa_sink_attention(heads(bq, SWA_HEADS), heads(bk, SWA_KV_HEADS), heads(bv, SWA_KV_HEADS),
                               q_gain, k_gain, sinks, alibi_slopes(SWA_HEADS))
    o = jnp.concatenate([a_out.transpose(0, 2, 1, 3).reshape(b, t, SB_W),
                         b_out.reshape(b, t, SWA_QW)], axis=-1)
    return o @ w_out


def causal_depthwise_conv(x, w):
    kk, c = w.shape
    return lax.conv_general_dilated(x, w[:, None, :].astype(x.dtype), window_strides=(1,),
                                    padding=[(kk - 1, 0)], dimension_numbers=('NWC', 'WIO', 'NWC'),
                                    feature_group_count=c)


def chunk_gated_delta_rule(q, k, v, g, beta):
    b, t, h, dk = q.shape
    dv = v.shape[-1]
    c = GDN_CHUNK
    n = t // c
    chunks = lambda z: z.astype(jnp.float32).reshape(b, n, c, h, -1).transpose(0, 3, 1, 2, 4)
    q, k, v = chunks(q), chunks(k), chunks(v)
    g = g.astype(jnp.float32).reshape(b, n, c, h).transpose(0, 3, 1, 2)
    beta = beta.astype(jnp.float32).reshape(b, n, c, h).transpose(0, 3, 1, 2)
    gc = jnp.cumsum(g, axis=-1)
    idx = jnp.arange(c)
    lower_incl = idx[:, None] >= idx[None, :]
    strict = idx[:, None] > idx[None, :]
    decay = jnp.exp(jnp.where(lower_incl, gc[..., :, None] - gc[..., None, :], -jnp.inf))
    kbeta = k * beta[..., None]
    lmat = jnp.where(strict, jnp.einsum('bhncd,bhnsd->bhncs', kbeta, k) * decay, 0.0)
    tmat = lmat + jnp.eye(c, dtype=jnp.float32)
    rhs = jnp.concatenate([v * beta[..., None], kbeta * jnp.exp(gc)[..., None]], axis=-1)
    sol = lax.linalg.triangular_solve(tmat, rhs, left_side=True, lower=True, unit_diagonal=True)
    u, w = sol[..., :dv], sol[..., dv:]
    attn = jnp.where(lower_incl, jnp.einsum('bhncd,bhnsd->bhncs', q, k) * decay, 0.0)
    q_dec = q * jnp.exp(gc)[..., None]
    k_tail = k * jnp.exp(gc[..., -1:] - gc)[..., None]
    chunk_dec = jnp.exp(gc[..., -1])

    def step(state, inp):
        u_c, w_c, qd_c, a_c, kt_c, dec_c = inp
        v_new = u_c - jnp.einsum('bhcd,bhdv->bhcv', w_c, state)
        o_c = jnp.einsum('bhcd,bhdv->bhcv', qd_c, state) + jnp.einsum('bhcs,bhsv->bhcv', a_c, v_new)
        state = state * dec_c[..., None, None] + jnp.einsum('bhcd,bhcv->bhdv', kt_c, v_new)
        return state, o_c

    xs = tuple(jnp.moveaxis(z, 2, 0) for z in (u, w, q_dec, attn, k_tail, chunk_dec))
    s0 = jnp.zeros((b, h, dk, dv), jnp.float32)
    _, o = lax.scan(step, s0, xs)
    return o.transpose(1, 0, 3, 2, 4).reshape(b, t, h, dv)


def gdn_mixer(h, w_in, conv_w, a_log, dt_bias, out_gain, w_out):
    b, t, _ = h.shape
    proj = h @ w_in
    qkv, z, beta_logit, a = jnp.split(
        proj, [GDN_CONV_W, GDN_CONV_W + GDN_VW, GDN_CONV_W + GDN_VW + GDN_V_HEADS], axis=-1)
    qkv = jax.nn.silu(causal_depthwise_conv(qkv, conv_w))
    q, k, v = jnp.split(qkv, [GDN_KW, 2 * GDN_KW], axis=-1)
    q = l2norm(q.reshape(b, t, GDN_K_HEADS, GDN_HEAD_DIM)) * (GDN_HEAD_DIM ** -0.5)
    k = l2norm(k.reshape(b, t, GDN_K_HEADS, GDN_HEAD_DIM))
    rep = GDN_V_HEADS // GDN_K_HEADS
    q = jnp.repeat(q, rep, axis=2)
    k = jnp.repeat(k, rep, axis=2)
    v = v.reshape(b, t, GDN_V_HEADS, GDN_HEAD_DIM)
    beta = jax.nn.sigmoid(beta_logit.astype(jnp.float32))
    g = -jnp.exp(a_log.astype(jnp.float32)) * jax.nn.softplus(a.astype(jnp.float32) + dt_bias.astype(jnp.float32))
    o = chunk_gated_delta_rule(q, k, v, g, beta).astype(h.dtype)
    o = rmsnorm(o, out_gain) * jax.nn.silu(z.reshape(b, t, GDN_V_HEADS, GDN_HEAD_DIM))
    return o.reshape(b, t, GDN_VW) @ w_out


def setup_inputs(seed: int = 0) -> dict:
    key = jax.random.key(seed)
    ks = jax.random.split(key, 24)
    f32 = jnp.float32
    nrm = lambda kk, shape, fan_in: jax.random.normal(kk, shape, f32) * (fan_in ** -0.5)
    gain = lambda kk, shape: 1.0 + 0.02 * jax.random.normal(kk, shape, f32)
    dt = jnp.exp(jax.random.uniform(ks[15], (N_ODD, GDN_V_HEADS), f32, math.log(1e-3), math.log(0.1)))
    return {
        'x': jax.random.normal(ks[0], (BATCH, SEQ, D_MODEL), f32),
        'p': jax.random.normal(ks[1], (DEPTH, BATCH, SEQ, PLE_DIM), f32),
        'ffn_norm': gain(ks[2], (DEPTH, 2, D_MODEL)),
        'ffn_w_gate': nrm(ks[3], (DEPTH, 2, D_MODEL, D_FF), D_MODEL),
        'ffn_w_up': nrm(ks[4], (DEPTH, 2, D_MODEL, D_FF), D_MODEL),
        'ffn_w_down': nrm(ks[5], (DEPTH, 2, D_FF, D_MODEL), D_FF),
        'mix_norm': gain(ks[6], (DEPTH, D_MODEL)),
        'att_w_in': nrm(ks[7], (N_EVEN, D_MODEL, ATT_IN), D_MODEL),
        'att_q_norm': gain(ks[8], (N_EVEN, HEAD_DIM)),
        'att_k_norm': gain(ks[9], (N_EVEN, HEAD_DIM)),
        'att_sinks': 0.5 * jax.random.normal(ks[10], (N_EVEN, SWA_HEADS), f32),
        'att_w_out': nrm(ks[11], (N_EVEN, ATT_OUT, D_MODEL), ATT_OUT),
        'gdn_w_in': nrm(ks[12], (N_ODD, D_MODEL, GDN_IN), D_MODEL),
        'gdn_conv_w': nrm(ks[13], (N_ODD, GDN_CONV, GDN_CONV_W), GDN_CONV),
        'gdn_a_log': jnp.log(jax.random.uniform(ks[14], (N_ODD, GDN_V_HEADS), f32, 1.0, 16.0)),
        'gdn_dt_bias': dt + jnp.log(-jnp.expm1(-dt)),
        'gdn_out_norm': gain(ks[16], (N_ODD, GDN_HEAD_DIM)),
        'gdn_w_out': nrm(ks[17], (N_ODD, GDN_VW, D_MODEL), GDN_VW),
        'ple_norm': gain(ks[18], (DEPTH, D_MODEL)),
        'ple_w_gate': nrm(ks[19], (DEPTH, D_MODEL, D_MODEL), D_MODEL),
        'ple_w_proj': nrm(ks[20], (DEPTH, PLE_DIM, D_MODEL), PLE_DIM),
    }


def reference(x, p, ffn_norm, ffn_w_gate, ffn_w_up, ffn_w_down, mix_norm,
              att_w_in, att_q_norm, att_k_norm, att_sinks, att_w_out,
              gdn_w_in, gdn_conv_w, gdn_a_log, gdn_dt_bias, gdn_out_norm, gdn_w_out,
              ple_norm, ple_w_gate, ple_w_proj):
    h = x
    for i in range(DEPTH):
        h = h + 0.5 * swiglu(rmsnorm(h, ffn_norm[i, 0]), ffn_w_gate[i, 0], ffn_w_up[i, 0], ffn_w_down[i, 0])
        hn = rmsnorm(h, mix_norm[i])
        j = i // 2
        if i % 2 == 0:
            h = h + attention_mixer(hn, att_w_in[j], att_q_norm[j], att_k_norm[j], att_sinks[j], att_w_out[j])
        else:
            h = h + gdn_mixer(hn, gdn_w_in[j], gdn_conv_w[j], gdn_a_log[j], gdn_dt_bias[j],
                              gdn_out_norm[j], gdn_w_out[j])
        h = h + 0.5 * swiglu(rmsnorm(h, ffn_norm[i, 1]), ffn_w_gate[i, 1], ffn_w_up[i, 1], ffn_w_down[i, 1])
        gate = jax.nn.sigmoid(rmsnorm(h, ple_norm[i]) @ ple_w_gate[i])
        h = h + gate * (p[i] @ ple_w_proj[i])
    return h
```

```python
import functools

import jax
import jax.numpy as jnp
from jax import lax
from jax.experimental import pallas as pl
from jax.experimental.pallas import tpu as pltpu

F32 = jnp.float32
BF16 = jnp.bfloat16

EPS = 1e-6
HEAD_DIM = 64
SB_HEADS = 8
SWA_HEADS = 8
SWA_KV_HEADS = 2
WINDOW = 128
GDN_K_HEADS = 8
GDN_V_HEADS = 16
GDN_HEAD_DIM = 128
GDN_CONV = 4
GDN_CHUNK = 64
SB_W = SB_HEADS * HEAD_DIM
SWA_QW = SWA_HEADS * HEAD_DIM
SWA_KVW = SWA_KV_HEADS * HEAD_DIM
GDN_KW = GDN_K_HEADS * GDN_HEAD_DIM
GDN_VW = GDN_V_HEADS * GDN_HEAD_DIM
GDN_CONV_W = 2 * GDN_KW + GDN_VW

LANES = 128
VMEM_LIMIT = 56 * 1024 * 1024

TOKEN_TILE = 512
FF_CHUNK = 256
SB_BLOCK = 256
GDN_STEP_CHUNKS = 4
INV_BLOCK = 16


def _params(sem):
    return pltpu.CompilerParams(dimension_semantics=sem, vmem_limit_bytes=VMEM_LIMIT)


def _dot(a, b):
    return jnp.dot(a, b, preferred_element_type=F32)


def _dot_nt(a, b):
    return lax.dot_general(a, b, (((1,), (1,)), ((), ())), preferred_element_type=F32)


def _dot_f32(a, b):
    return jnp.dot(a, b, preferred_element_type=F32, precision=lax.Precision.HIGHEST)


def _rms_scale(x):
    return lax.rsqrt(jnp.mean(x * x, axis=-1, keepdims=True) + EPS)


def _sigmoid(x):
    return 1.0 / (1.0 + jnp.exp(-x))


def _softplus(x):
    return jnp.maximum(x, 0.0) + jnp.log1p(jnp.exp(-jnp.abs(x)))


def _ffn_body(x_ref, g_ref, wg_ref, wu_ref, wd_ref, o_ref, acc_ref, *, n_chunks, fc):
    x = x_ref[...]
    xn = (x * _rms_scale(x) * g_ref[...]).astype(BF16)
    for c in range(n_chunks):
        cols = slice(c * fc, (c + 1) * fc)
        gate = _dot(xn, wg_ref[:, cols])
        up = _dot(xn, wu_ref[:, cols])
        act = (gate * _sigmoid(gate) * up).astype(BF16)
        part = _dot(act, wd_ref[cols, :])
        if c == 0:
            acc_ref[...] = part
        else:
            acc_ref[...] += part
    o_ref[...] = x + 0.5 * acc_ref[...]


def _ffn(h, gain, wg, wu, wd, tm):
    n, d = h.shape
    f = wg.shape[1]
    fc = FF_CHUNK if f % FF_CHUNK == 0 else f
    const = lambda i: (0, 0)
    return pl.pallas_call(
        functools.partial(_ffn_body, n_chunks=f // fc, fc=fc),
        out_shape=jax.ShapeDtypeStruct((n, d), F32),
        grid=(n // tm,),
        in_specs=[
            pl.BlockSpec((tm, d), lambda i: (i, 0)),
            pl.BlockSpec((1, d), const),
            pl.BlockSpec((d, f), const, pipeline_mode=pl.Buffered(1)),
            pl.BlockSpec((d, f), const, pipeline_mode=pl.Buffered(1)),
            pl.BlockSpec((f, d), const, pipeline_mode=pl.Buffered(1)),
        ],
        out_specs=pl.BlockSpec((tm, d), lambda i: (i, 0)),
        scratch_shapes=[pltpu.VMEM((tm, d), F32)],
        compiler_params=_params(("parallel",)),
        name="ffn",
    )(h, gain.reshape(1, d), wg, wu, wd)


def _norm_proj_body(x_ref, g_ref, w_ref, *o_refs, widths):
    x = x_ref[...]
    xn = (x * _rms_scale(x) * g_ref[...]).astype(BF16)
    start = 0
    for o_ref, width in zip(o_refs, widths):
        o_ref[...] = _dot(xn, w_ref[:, start:start + width])
        start += width


def _norm_proj(h, gain, w, widths, tm):
    n, d = h.shape
    assert sum(widths) == w.shape[1]
    return pl.pallas_call(
        functools.partial(_norm_proj_body, widths=widths),
        out_shape=[jax.ShapeDtypeStruct((n, width), F32) for width in widths],
        grid=(n // tm,),
        in_specs=[
            pl.BlockSpec((tm, d), lambda i: (i, 0)),
            pl.BlockSpec((1, d), lambda i: (0, 0)),
            pl.BlockSpec(w.shape, lambda i: (0, 0), pipeline_mode=pl.Buffered(1)),
        ],
        out_specs=[pl.BlockSpec((tm, width), lambda i: (i, 0)) for width in widths],
        compiler_params=_params(("parallel",)),
        name="norm_proj",
    )(h, gain.reshape(1, d), w)


def _att_out_body(h_ref, a_ref, b_ref, wa_ref, wb_ref, o_ref):
    o_ref[...] = (h_ref[...] + _dot(a_ref[...].astype(BF16), wa_ref[...])
                  + _dot(b_ref[...].astype(BF16), wb_ref[...]))


def _att_out(h, a, b, wa, wb, tm):
    n, d = h.shape
    row = lambda i: (i, 0)
    const = lambda i: (0, 0)
    return pl.pallas_call(
        _att_out_body,
        out_shape=jax.ShapeDtypeStruct((n, d), F32),
        grid=(n // tm,),
        in_specs=[
            pl.BlockSpec((tm, d), row),
            pl.BlockSpec((tm, a.shape[1]), row),
            pl.BlockSpec((tm, b.shape[1]), row),
            pl.BlockSpec(wa.shape, const, pipeline_mode=pl.Buffered(1)),
            pl.BlockSpec(wb.shape, const, pipeline_mode=pl.Buffered(1)),
        ],
        out_specs=pl.BlockSpec((tm, d), row),
        compiler_params=_params(("parallel",)),
        name="att_out",
    )(h, a, b, wa, wb)


def _ple_body(x_ref, p_ref, g_ref, wg_ref, wp_ref, o_ref):
    x = x_ref[...]
    xn = (x * _rms_scale(x) * g_ref[...]).astype(BF16)
    gate = _sigmoid(_dot(xn, wg_ref[...]))
    o_ref[...] = x + gate * _dot(p_ref[...].astype(BF16), wp_ref[...])


def _ple(h, p, gain, wg, wp, tm):
    n, d = h.shape
    row = lambda i: (i, 0)
    const = lambda i: (0, 0)
    return pl.pallas_call(
        _ple_body,
        out_shape=jax.ShapeDtypeStruct((n, d), F32),
        grid=(n // tm,),
        in_specs=[
            pl.BlockSpec((tm, d), row),
            pl.BlockSpec((tm, p.shape[1]), row),
            pl.BlockSpec((1, d), const),
            pl.BlockSpec(wg.shape, const, pipeline_mode=pl.Buffered(1)),
            pl.BlockSpec(wp.shape, const, pipeline_mode=pl.Buffered(1)),
        ],
        out_specs=pl.BlockSpec((tm, d), row),
        compiler_params=_params(("parallel",)),
        name="ple",
    )(h, p, gain.reshape(1, d), wg, wp)


def _sb_body(q_ref, k_ref, v_ref, o_ref, *, blk, scale):
    qi = pl.program_id(2)
    row = lax.broadcasted_iota(jnp.int32, (blk, blk), 0)
    col = lax.broadcasted_iota(jnp.int32, (blk, blk), 1)
    causal = col < row
    later = jnp.where(row > col, 1.0, 0.0).astype(BF16)

    def rev_cumsum(x):
        hi = x.astype(BF16)
        lo = (x - hi.astype(F32)).astype(BF16)
        return _dot(hi, later) + _dot(lo, later)

    outs = []
    for head in range(LANES // HEAD_DIM):
        lanes = slice(head * HEAD_DIM, (head + 1) * HEAD_DIM)
        q = q_ref[0, :, lanes].astype(BF16)

        def block(j, masked):
            start = pl.multiple_of(j * blk, blk)
            k = k_ref[0, pl.ds(start, blk), lanes].astype(BF16)
            v = v_ref[0, pl.ds(start, blk), lanes].astype(BF16)
            z = _dot_nt(q, k) * scale
            sp = _softplus(z)
            log_keep = jnp.where(causal, -sp, 0.0) if masked else -sp
            log_w = (z - sp) + rev_cumsum(log_keep)
            return log_w, log_keep, v

        log_w, log_keep, v = block(qi, True)
        w = jnp.where(causal, jnp.exp(log_w), 0.0)
        acc = _dot(w.astype(BF16), v)
        carry = jnp.sum(log_keep, axis=-1, keepdims=True)

        def step(it, state):
            acc, carry = state
            log_w, log_keep, v = block(qi - 1 - it, False)
            w = jnp.exp(log_w + carry)
            acc = acc + _dot(w.astype(BF16), v)
            carry = carry + jnp.sum(log_keep, axis=-1, keepdims=True)
            return acc, carry

        acc, _ = lax.fori_loop(0, qi, step, (acc, carry))
        outs.append(acc)
    o_ref[0] = jnp.concatenate(outs, axis=-1)


def _sb_attention(proj, blk):
    b, t, _ = proj.shape
    pairs = SB_W // LANES
    return pl.pallas_call(
        functools.partial(_sb_body, blk=blk, scale=HEAD_DIM ** -0.5),
        out_shape=jax.ShapeDtypeStruct((b, t, SB_W), F32),
        grid=(b, pairs, t // blk),
        in_specs=[
            pl.BlockSpec((1, blk, LANES), lambda bi, hp, qi: (bi, qi, hp)),
            pl.BlockSpec((1, t, LANES), lambda bi, hp, qi: (bi, 0, pairs + hp)),
            pl.BlockSpec((1, t, LANES), lambda bi, hp, qi: (bi, 0, 2 * pairs + hp)),
        ],
        out_specs=pl.BlockSpec((1, blk, LANES), lambda bi, hp, qi: (bi, qi, hp)),
        compiler_params=_params(("parallel", "parallel", "arbitrary")),
        name="sb_attention",
    )(proj, proj, proj)


def _swa_body(sink_ref, q_ref, kvc_ref, kvp_ref, qg_ref, kg_ref, o_ref, *, scale, slopes):
    blk = pl.program_id(1)
    w = WINDOW
    q = q_ref[0]
    kv = jnp.concatenate([kvp_ref[0], kvc_ref[0]], axis=0)
    qpos = lax.broadcasted_iota(jnp.int32, (w, 2 * w), 0)
    kpos = lax.broadcasted_iota(jnp.int32, (w, 2 * w), 1)
    dist = qpos + w - kpos
    valid = (dist >= 0) & (dist < w) & ((blk > 0) | (kpos >= w))
    distf = dist.astype(F32)
    group = SWA_HEADS // SWA_KV_HEADS
    outs = []
    for kvh in range(SWA_KV_HEADS):
        k = kv[:, kvh * HEAD_DIM:(kvh + 1) * HEAD_DIM]
        k = (k * _rms_scale(k) * kg_ref[...]).astype(BF16)
        v = kv[:, SWA_KVW + kvh * HEAD_DIM:SWA_KVW + (kvh + 1) * HEAD_DIM].astype(BF16)
        for g in range(group):
            head = kvh * group + g
            qh = q[:, head * HEAD_DIM:(head + 1) * HEAD_DIM]
            qh = (qh * _rms_scale(qh) * qg_ref[...]).astype(BF16)
            s = _dot_nt(qh, k) * scale - slopes[head] * distf
            s = jnp.where(valid, s, -jnp.inf)
            sink = sink_ref[head]
            m = jnp.maximum(jnp.max(s, axis=-1, keepdims=True), sink)
            e = jnp.exp(s - m)
            denom = jnp.sum(e, axis=-1, keepdims=True) + jnp.exp(sink - m)
            outs.append(_dot((e / denom).astype(BF16), v))
    o_ref[0] = jnp.concatenate(outs, axis=-1)


def _swa_attention(proj, q_gain, k_gain, sinks):
    b, t, _ = proj.shape
    w = WINDOW
    q_blk = 3 * SB_W // SWA_QW
    kv_blk = (3 * SB_W + SWA_QW) // (2 * SWA_KVW)
    slopes = tuple(float(2.0 ** (-8.0 * (i + 1) / SWA_HEADS)) for i in range(SWA_HEADS))
    return pl.pallas_call(
        functools.partial(_swa_body, scale=HEAD_DIM ** -0.5, slopes=slopes),
        out_shape=jax.ShapeDtypeStruct((b, t, SWA_QW), F32),
        grid=(b, t // w),
        in_specs=[
            pl.BlockSpec(memory_space=pltpu.SMEM),
            pl.BlockSpec((1, w, SWA_QW), lambda bi, i: (bi, i, q_blk)),
            pl.BlockSpec((1, w, 2 * SWA_KVW), lambda bi, i: (bi, i, kv_blk)),
            pl.BlockSpec((1, w, 2 * SWA_KVW), lambda bi, i: (bi, jnp.maximum(i - 1, 0), kv_blk)),
            pl.BlockSpec((1, HEAD_DIM), lambda bi, i: (0, 0)),
            pl.BlockSpec((1, HEAD_DIM), lambda bi, i: (0, 0)),
        ],
        out_specs=pl.BlockSpec((1, w, SWA_QW), lambda bi, i: (bi, i, 0)),
        compiler_params=_params(("parallel", "parallel")),
        name="swa_attention",
    )(sinks, proj, proj, proj, q_gain.reshape(1, HEAD_DIM), k_gain.reshape(1, HEAD_DIM))


def _gdn_prep_body(x_ref, halo_ref, cw_ref, ba_ref, alog_ref, dtb_ref, o_ref, gb_ref, ext_ref, *, tm):
    ti = pl.program_id(1)
    ci = pl.program_id(2)
    halo = halo_ref[0]
    ext_ref[0:8, :] = jnp.where(ti > 0, halo, jnp.zeros_like(halo))
    ext_ref[8:, :] = x_ref[0]
    y = None
    for tap in range(GDN_CONV):
        off = 8 - (GDN_CONV - 1) + tap
        term = ext_ref[off:off + tm, :] * cw_ref[tap:tap + 1, :]
        y = term if y is None else y + term
    y = y * _sigmoid(y)
    q_blocks = GDN_KW // o_ref.shape[2]
    is_q = ci < q_blocks
    is_k = (ci >= q_blocks) & (ci < 2 * q_blocks)
    for head in range(o_ref.shape[2] // GDN_HEAD_DIM):
        lanes = slice(head * GDN_HEAD_DIM, (head + 1) * GDN_HEAD_DIM)
        yh = y[:, lanes]
        inv = lax.rsqrt(jnp.sum(yh * yh, axis=-1, keepdims=True) + EPS)
        mult = jnp.where(is_q, inv * (GDN_HEAD_DIM ** -0.5), jnp.where(is_k, inv, 1.0))
        o_ref[0, :, lanes] = yh * mult

    @pl.when(ci == 0)
    def _():
        ba = ba_ref[0]
        beta = _sigmoid(ba[:, :GDN_V_HEADS])
        g = -jnp.exp(alog_ref[...]) * _softplus(ba[:, GDN_V_HEADS:2 * GDN_V_HEADS] + dtb_ref[...])
        r = lax.broadcasted_iota(jnp.int32, (GDN_CHUNK, GDN_CHUNK), 0)
        c = lax.broadcasted_iota(jnp.int32, (GDN_CHUNK, GDN_CHUNK), 1)
        lower = jnp.where(r >= c, 1.0, 0.0).astype(F32)
        for chunk in range(tm // GDN_CHUNK):
            rows = slice(chunk * GDN_CHUNK, (chunk + 1) * GDN_CHUNK)
            gc = _dot_f32(lower, g[rows])
            gb_ref[0, rows, :] = jnp.concatenate([gc, beta[rows]], axis=-1)


def _gdn_prep(qkv, ba, conv_w, a_log, dt_bias, tm, cw):
    b, t, c = qkv.shape
    nh = GDN_V_HEADS
    return pl.pallas_call(
        functools.partial(_gdn_prep_body, tm=tm),
        out_shape=[jax.ShapeDtypeStruct((b, t, c), F32), jax.ShapeDtypeStruct((b, t, 2 * nh), F32)],
        grid=(b, t // tm, c // cw),
        in_specs=[
            pl.BlockSpec((1, tm, cw), lambda bi, ti, ci: (bi, ti, ci)),
            pl.BlockSpec((1, 8, cw), lambda bi, ti, ci: (bi, jnp.maximum(ti * (tm // 8) - 1, 0), ci)),
            pl.BlockSpec((GDN_CONV, cw), lambda bi, ti, ci: (0, ci)),
            pl.BlockSpec((1, tm, 2 * nh), lambda bi, ti, ci: (bi, ti, 0)),
            pl.BlockSpec((1, nh), lambda bi, ti, ci: (0, 0)),
            pl.BlockSpec((1, nh), lambda bi, ti, ci: (0, 0)),
        ],
        out_specs=[
            pl.BlockSpec((1, tm, cw), lambda bi, ti, ci: (bi, ti, ci)),
            pl.BlockSpec((1, tm, 2 * nh), lambda bi, ti, ci: (bi, ti, 0)),
        ],
        scratch_shapes=[pltpu.VMEM((tm + 8, cw), F32)],
        compiler_params=_params(("parallel", "parallel", "arbitrary")),
        name="gdn_prep",
    )(qkv, qkv, conv_w, ba, a_log.reshape(1, nh), dt_bias.reshape(1, nh))


def _unit_lower_inverse(a, ii, jj):
    c = a.shape[0]
    eye = jnp.where(ii == jj, 1.0, 0.0).astype(F32)
    same = (ii // INV_BLOCK) == (jj // INV_BLOCK)
    ad = jnp.where(same, a, 0.0)
    lo = a - ad
    dinv = eye - ad
    power = ad
    span = 2
    while span < INV_BLOCK:
        power = _dot_f32(power, power)
        dinv = dinv + _dot_f32(dinv, power)
        span *= 2
    n = _dot_f32(dinv, lo)
    m = eye - n
    power = n
    span = 2
    while span < c // INV_BLOCK:
        power = _dot_f32(power, power)
        m = m + _dot_f32(m, power)
        span *= 2
    return _dot_f32(m, dinv)


def _gdn_body(q_ref, k_ref, v_ref, gb_ref, gr_ref, o_ref, s_ref, *, n_chunks):
    head = pl.program_id(1)
    cs = GDN_CHUNK
    nh = GDN_V_HEADS

    @pl.when(pl.program_id(2) == 0)
    def _():
        s_ref[...] = jnp.zeros_like(s_ref)

    ii = lax.broadcasted_iota(jnp.int32, (cs, cs), 0)
    jj = lax.broadcasted_iota(jnp.int32, (cs, cs), 1)
    lane = lax.broadcasted_iota(jnp.int32, (cs, 2 * nh), 1)
    state = s_ref[...]
    for c in range(n_chunks):
        rows = slice(c * cs, (c + 1) * cs)
        q = q_ref[0, rows, :]
        k = k_ref[0, rows, :]
        v = v_ref[0, rows, :]
        gb = gb_ref[0, rows, :]
        gc_col = jnp.sum(jnp.where(lane == head, gb, 0.0), axis=-1, keepdims=True)
        beta = jnp.sum(jnp.where(lane == nh + head, gb, 0.0), axis=-1, keepdims=True)
        gc_row = gr_ref[0, 0, c]
        decay = jnp.exp(jnp.where(ii >= jj, gc_col - gc_row, -jnp.inf))
        kb = k * beta
        k16 = k.astype(BF16)
        a = jnp.where(ii > jj, _dot_nt(kb.astype(BF16), k16) * decay, 0.0)
        tinv = _unit_lower_inverse(a, ii, jj)
        eg = jnp.exp(gc_col)
        rhs = jnp.concatenate([v * beta, kb * eg], axis=-1)
        sol = _dot_f32(tinv, rhs)
        u = sol[:, :GDN_HEAD_DIM]
        w = sol[:, GDN_HEAD_DIM:]
        attn = jnp.where(ii >= jj, _dot_nt(q.astype(BF16), k16) * decay, 0.0)
        g_last = gc_row[:, cs - 1:cs]
        k_tail = k * jnp.exp(g_last - gc_col)
        s16 = state.astype(BF16)
        v_new = u - _dot(w.astype(BF16), s16)
        v16 = v_new.astype(BF16)
        o_ref[0, rows, :] = _dot((q * eg).astype(BF16), s16) + _dot(attn.astype(BF16), v16)
        state = state * jnp.exp(g_last) + _dot(k_tail.T.astype(BF16), v16)
    s_ref[...] = state


def _gdn_recurrence(qkv, gb, gc_rows, n_chunks):
    b, t, _ = qkv.shape
    nh = GDN_V_HEADS
    rep = GDN_V_HEADS // GDN_K_HEADS
    tt = n_chunks * GDN_CHUNK
    d = GDN_HEAD_DIM
    return pl.pallas_call(
        functools.partial(_gdn_body, n_chunks=n_chunks),
        out_shape=jax.ShapeDtypeStruct((b, t, GDN_VW), F32),
        grid=(b, nh, t // tt),
        in_specs=[
            pl.BlockSpec((1, tt, d), lambda bi, h, ti: (bi, ti, h // rep)),
            pl.BlockSpec((1, tt, d), lambda bi, h, ti: (bi, ti, GDN_K_HEADS + h // rep)),
            pl.BlockSpec((1, tt, d), lambda bi, h, ti: (bi, ti, 2 * GDN_K_HEADS + h)),
            pl.BlockSpec((1, tt, 2 * nh), lambda bi, h, ti: (bi, ti, 0)),
            pl.BlockSpec((1, 1, n_chunks, 1, GDN_CHUNK), lambda bi, h, ti: (bi, h, ti, 0, 0)),
        ],
        out_specs=pl.BlockSpec((1, tt, d), lambda bi, h, ti: (bi, ti, h)),
        scratch_shapes=[pltpu.VMEM((d, d), F32)],
        compiler_params=_params(("parallel", "parallel", "arbitrary")),
        name="gdn_recurrence",
    )(qkv, qkv, qkv, gb, gc_rows)


def _gdn_out_body(h_ref, o_ref_in, z_ref, g_ref, w_ref, out_ref, y_ref):
    for head in range(GDN_V_HEADS):
        lanes = slice(head * GDN_HEAD_DIM, (head + 1) * GDN_HEAD_DIM)
        o = o_ref_in[:, lanes]
        z = z_ref[:, lanes]
        y_ref[:, lanes] = (o * _rms_scale(o) * g_ref[...] * (z * _sigmoid(z))).astype(BF16)
    out_ref[...] = h_ref[...] + _dot(y_ref[...], w_ref[...])


def _gdn_out(h, o, z, gain, w, tm):
    n, d = h.shape
    row = lambda i: (i, 0)
    const = lambda i: (0, 0)
    return pl.pallas_call(
        _gdn_out_body,
        out_shape=jax.ShapeDtypeStruct((n, d), F32),
        grid=(n // tm,),
        in_specs=[
            pl.BlockSpec((tm, d), row),
            pl.BlockSpec((tm, GDN_VW), row),
            pl.BlockSpec((tm, GDN_VW), row),
            pl.BlockSpec((1, GDN_HEAD_DIM), const),
            pl.BlockSpec(w.shape, const, pipeline_mode=pl.Buffered(1)),
        ],
        out_specs=pl.BlockSpec((tm, d), row),
        scratch_shapes=[pltpu.VMEM((tm, GDN_VW), BF16)],
        compiler_params=_params(("parallel",)),
        name="gdn_out",
    )(h, o, z, gain.reshape(1, GDN_HEAD_DIM), w)


def _attention_mixer(h, b, t, gain, w_in, q_gain, k_gain, sinks, w_out, tm):
    d = h.shape[1]
    (proj,) = _norm_proj(h, gain, w_in.astype(BF16), (w_in.shape[1],), tm)
    proj = proj.reshape(b, t, -1)
    a_out = _sb_attention(proj, min(SB_BLOCK, t)).reshape(b * t, SB_W)
    b_out = _swa_attention(proj, q_gain, k_gain, sinks).reshape(b * t, SWA_QW)
    w16 = w_out.astype(BF16)
    return _att_out(h, a_out, b_out, w16[:SB_W], w16[SB_W:], tm)


def _gdn_mixer(h, b, t, gain, w_in, conv_w, a_log, dt_bias, out_gain, w_out, tm):
    nh = GDN_V_HEADS
    ba_w = w_in.shape[1] - GDN_CONV_W - GDN_VW
    pad = (-ba_w) % LANES
    w16 = jnp.pad(w_in.astype(BF16), ((0, 0), (0, pad)))
    qkv, z, ba = _norm_proj(h, gain, w16, (GDN_CONV_W, GDN_VW, ba_w + pad), tm)
    ba = ba[:, :ba_w].reshape(b, t, ba_w)
    qkv, gb = _gdn_prep(qkv.reshape(b, t, GDN_CONV_W), ba, conv_w, a_log, dt_bias,
                        min(tm, t), 1024)
    gc_rows = jnp.transpose(gb[:, :, :nh], (0, 2, 1)).reshape(b, nh, t // GDN_CHUNK, 1, GDN_CHUNK)
    o = _gdn_recurrence(qkv, gb, gc_rows, GDN_STEP_CHUNKS)
    return _gdn_out(h, o.reshape(b * t, GDN_VW), z, out_gain, w_out.astype(BF16), tm)


def kernel(x, p, ffn_norm, ffn_w_gate, ffn_w_up, ffn_w_down, mix_norm, att_w_in, att_q_norm, att_k_norm, att_sinks, att_w_out, gdn_w_in, gdn_conv_w, gdn_a_log, gdn_dt_bias, gdn_out_norm, gdn_w_out, ple_norm, ple_w_gate, ple_w_proj):
    b, t, d = x.shape
    n = b * t
    tm = min(TOKEN_TILE, n)
    depth = p.shape[0]
    h = x.reshape(n, d)
    for i in range(depth):
        j = i // 2
        h = _ffn(h, ffn_norm[i, 0], ffn_w_gate[i, 0].astype(BF16), ffn_w_up[i, 0].astype(BF16),
                 ffn_w_down[i, 0].astype(BF16), tm)
        if i % 2 == 0:
            h = _attention_mixer(h, b, t, mix_norm[i], att_w_in[j], att_q_norm[j], att_k_norm[j],
                                 att_sinks[j], att_w_out[j], tm)
        else:
            h = _gdn_mixer(h, b, t, mix_norm[i], gdn_w_in[j], gdn_conv_w[j], gdn_a_log[j],
                           gdn_dt_bias[j], gdn_out_norm[j], gdn_w_out[j], tm)
        h = _ffn(h, ffn_norm[i, 1], ffn_w_gate[i, 1].astype(BF16), ffn_w_up[i, 1].astype(BF16),
                 ffn_w_down[i, 1].astype(BF16), tm)
        h = _ple(h, p[i].reshape(n, -1), ple_norm[i], ple_w_gate[i].astype(BF16),
                 ple_w_proj[i].astype(BF16), tm)
    return h.reshape(b, t, d)
```

```python
import functools

import jax
import jax.numpy as jnp
from jax import lax
from jax.experimental import pallas as pl
from jax.experimental.pallas import tpu as pltpu

F32 = jnp.float32
BF16 = jnp.bfloat16

EPS = 1e-6
HEAD_DIM = 64
SB_HEADS = 8
SWA_HEADS = 8
SWA_KV_HEADS = 2
WINDOW = 128
GDN_K_HEADS = 8
GDN_V_HEADS = 16
GDN_HEAD_DIM = 128
GDN_CONV = 4
GDN_CHUNK = 64
SB_W = SB_HEADS * HEAD_DIM
SWA_QW = SWA_HEADS * HEAD_DIM
SWA_KVW = SWA_KV_HEADS * HEAD_DIM
GDN_KW = GDN_K_HEADS * GDN_HEAD_DIM
GDN_VW = GDN_V_HEADS * GDN_HEAD_DIM
GDN_CONV_W = 2 * GDN_KW + GDN_VW

LANES = 128
VMEM_LIMIT = 56 * 1024 * 1024

TOKEN_TILE = 512
FF_CHUNK = 256
SB_BLOCK = 256
GDN_LOCAL_CHUNKS = 8
GDN_SCAN_CHUNKS = 4
GDN_SCAN_PAIRS = 4
INV_BLOCK = 16


def _params(sem):
    return pltpu.CompilerParams(dimension_semantics=sem, vmem_limit_bytes=VMEM_LIMIT)


def _dot(a, b):
    return jnp.dot(a, b, preferred_element_type=F32)


def _dot_nt(a, b):
    return lax.dot_general(a, b, (((1,), (1,)), ((), ())), preferred_element_type=F32)


def _dot_f32(a, b):
    return jnp.dot(a, b, preferred_element_type=F32, precision=lax.Precision.HIGHEST)


def _rms_scale(x):
    return lax.rsqrt(jnp.mean(x * x, axis=-1, keepdims=True) + EPS)


def _sigmoid(x):
    return 1.0 / (1.0 + jnp.exp(-x))


def _softplus(x):
    return jnp.maximum(x, 0.0) + jnp.log1p(jnp.exp(-jnp.abs(x)))


def _ffn_body(x_ref, g_ref, wg_ref, wu_ref, wd_ref, o_ref, acc_ref, *, n_chunks, fc):
    x = x_ref[...]
    xn = (x * _rms_scale(x) * g_ref[...]).astype(BF16)
    for c in range(n_chunks):
        cols = slice(c * fc, (c + 1) * fc)
        gate = _dot(xn, wg_ref[:, cols])
        up = _dot(xn, wu_ref[:, cols])
        act = (gate * _sigmoid(gate) * up).astype(BF16)
        part = _dot(act, wd_ref[cols, :])
        if c == 0:
            acc_ref[...] = part
        else:
            acc_ref[...] += part
    o_ref[...] = x + 0.5 * acc_ref[...]


def _ffn(h, gain, wg, wu, wd, tm):
    n, d = h.shape
    f = wg.shape[1]
    fc = FF_CHUNK if f % FF_CHUNK == 0 else f
    const = lambda i: (0, 0)
    return pl.pallas_call(
        functools.partial(_ffn_body, n_chunks=f // fc, fc=fc),
        out_shape=jax.ShapeDtypeStruct((n, d), F32),
        grid=(n // tm,),
        in_specs=[
            pl.BlockSpec((tm, d), lambda i: (i, 0)),
            pl.BlockSpec((1, d), const),
            pl.BlockSpec((d, f), const, pipeline_mode=pl.Buffered(1)),
            pl.BlockSpec((d, f), const, pipeline_mode=pl.Buffered(1)),
            pl.BlockSpec((f, d), const, pipeline_mode=pl.Buffered(1)),
        ],
        out_specs=pl.BlockSpec((tm, d), lambda i: (i, 0)),
        scratch_shapes=[pltpu.VMEM((tm, d), F32)],
        compiler_params=_params(("parallel",)),
        name="ffn",
    )(h, gain.reshape(1, d), wg, wu, wd)


def _norm_proj_body(x_ref, g_ref, w_ref, *o_refs, widths):
    x = x_ref[...]
    xn = (x * _rms_scale(x) * g_ref[...]).astype(BF16)
    start = 0
    for o_ref, width in zip(o_refs, widths):
        o_ref[...] = _dot(xn, w_ref[:, start:start + width])
        start += width


def _norm_proj(h, gain, w, widths, tm):
    n, d = h.shape
    assert sum(widths) == w.shape[1]
    return pl.pallas_call(
        functools.partial(_norm_proj_body, widths=widths),
        out_shape=[jax.ShapeDtypeStruct((n, width), F32) for width in widths],
        grid=(n // tm,),
        in_specs=[
            pl.BlockSpec((tm, d), lambda i: (i, 0)),
            pl.BlockSpec((1, d), lambda i: (0, 0)),
            pl.BlockSpec(w.shape, lambda i: (0, 0), pipeline_mode=pl.Buffered(1)),
        ],
        out_specs=[pl.BlockSpec((tm, width), lambda i: (i, 0)) for width in widths],
        compiler_params=_params(("parallel",)),
        name="norm_proj",
    )(h, gain.reshape(1, d), w)


def _att_out_body(h_ref, a_ref, b_ref, wa_ref, wb_ref, o_ref):
    o_ref[...] = (h_ref[...] + _dot(a_ref[...].astype(BF16), wa_ref[...])
                  + _dot(b_ref[...].astype(BF16), wb_ref[...]))


def _att_out(h, a, b, wa, wb, tm):
    n, d = h.shape
    row = lambda i: (i, 0)
    const = lambda i: (0, 0)
    return pl.pallas_call(
        _att_out_body,
        out_shape=jax.ShapeDtypeStruct((n, d), F32),
        grid=(n // tm,),
        in_specs=[
            pl.BlockSpec((tm, d), row),
            pl.BlockSpec((tm, a.shape[1]), row),
            pl.BlockSpec((tm, b.shape[1]), row),
            pl.BlockSpec(wa.shape, const, pipeline_mode=pl.Buffered(1)),
            pl.BlockSpec(wb.shape, const, pipeline_mode=pl.Buffered(1)),
        ],
        out_specs=pl.BlockSpec((tm, d), row),
        compiler_params=_params(("parallel",)),
        name="att_out",
    )(h, a, b, wa, wb)


def _ple_body(x_ref, p_ref, g_ref, wg_ref, wp_ref, o_ref):
    x = x_ref[...]
    xn = (x * _rms_scale(x) * g_ref[...]).astype(BF16)
    gate = _sigmoid(_dot(xn, wg_ref[...]))
    o_ref[...] = x + gate * _dot(p_ref[...].astype(BF16), wp_ref[...])


def _ple(h, p, gain, wg, wp, tm):
    n, d = h.shape
    row = lambda i: (i, 0)
    const = lambda i: (0, 0)
    return pl.pallas_call(
        _ple_body,
        out_shape=jax.ShapeDtypeStruct((n, d), F32),
        grid=(n // tm,),
        in_specs=[
            pl.BlockSpec((tm, d), row),
            pl.BlockSpec((tm, p.shape[1]), row),
            pl.BlockSpec((1, d), const),
            pl.BlockSpec(wg.shape, const, pipeline_mode=pl.Buffered(1)),
            pl.BlockSpec(wp.shape, const, pipeline_mode=pl.Buffered(1)),
        ],
        out_specs=pl.BlockSpec((tm, d), row),
        compiler_params=_params(("parallel",)),
        name="ple",
    )(h, p, gain.reshape(1, d), wg, wp)


def _sb_body(q_ref, k_ref, v_ref, o_ref, *, blk, scale):
    qi = pl.program_id(2)
    row = lax.broadcasted_iota(jnp.int32, (blk, blk), 0)
    col = lax.broadcasted_iota(jnp.int32, (blk, blk), 1)
    causal = col < row
    later = jnp.where(row > col, 1.0, 0.0).astype(BF16)

    def rev_cumsum(x):
        hi = x.astype(BF16)
        lo = (x - hi.astype(F32)).astype(BF16)
        return _dot(hi, later) + _dot(lo, later)

    outs = []
    for head in range(LANES // HEAD_DIM):
        lanes = slice(head * HEAD_DIM, (head + 1) * HEAD_DIM)
        q = q_ref[0, :, lanes].astype(BF16)

        def block(j, masked):
            start = pl.multiple_of(j * blk, blk)
            k = k_ref[0, pl.ds(start, blk), lanes].astype(BF16)
            v = v_ref[0, pl.ds(start, blk), lanes].astype(BF16)
            z = _dot_nt(q, k) * scale
            sp = _softplus(z)
            log_keep = jnp.where(causal, -sp, 0.0) if masked else -sp
            log_w = (z - sp) + rev_cumsum(log_keep)
            return log_w, log_keep, v

        log_w, log_keep, v = block(qi, True)
        w = jnp.where(causal, jnp.exp(log_w), 0.0)
        acc = _dot(w.astype(BF16), v)
        carry = jnp.sum(log_keep, axis=-1, keepdims=True)

        def step(it, state):
            acc, carry = state
            log_w, log_keep, v = block(qi - 1 - it, False)
            w = jnp.exp(log_w + carry)
            acc = acc + _dot(w.astype(BF16), v)
            carry = carry + jnp.sum(log_keep, axis=-1, keepdims=True)
            return acc, carry

        acc, _ = lax.fori_loop(0, qi, step, (acc, carry))
        outs.append(acc)
    o_ref[0] = jnp.concatenate(outs, axis=-1)


def _sb_attention(proj, blk):
    b, t, _ = proj.shape
    pairs = SB_W // LANES
    return pl.pallas_call(
        functools.partial(_sb_body, blk=blk, scale=HEAD_DIM ** -0.5),
        out_shape=jax.ShapeDtypeStruct((b, t, SB_W), F32),
        grid=(b, pairs, t // blk),
        in_specs=[
            pl.BlockSpec((1, blk, LANES), lambda bi, hp, qi: (bi, qi, hp)),
            pl.BlockSpec((1, t, LANES), lambda bi, hp, qi: (bi, 0, pairs + hp)),
            pl.BlockSpec((1, t, LANES), lambda bi, hp, qi: (bi, 0, 2 * pairs + hp)),
        ],
        out_specs=pl.BlockSpec((1, blk, LANES), lambda bi, hp, qi: (bi, qi, hp)),
        compiler_params=_params(("parallel", "parallel", "arbitrary")),
        name="sb_attention",
    )(proj, proj, proj)


def _swa_body(sink_ref, q_ref, kvc_ref, kvp_ref, qg_ref, kg_ref, o_ref, *, scale, slopes):
    blk = pl.program_id(1)
    w = WINDOW
    q = q_ref[0]
    kv = jnp.concatenate([kvp_ref[0], kvc_ref[0]], axis=0)
    qpos = lax.broadcasted_iota(jnp.int32, (w, 2 * w), 0)
    kpos = lax.broadcasted_iota(jnp.int32, (w, 2 * w), 1)
    dist = qpos + w - kpos
    valid = (dist >= 0) & (dist < w) & ((blk > 0) | (kpos >= w))
    distf = dist.astype(F32)
    group = SWA_HEADS // SWA_KV_HEADS
    outs = []
    for kvh in range(SWA_KV_HEADS):
        k = kv[:, kvh * HEAD_DIM:(kvh + 1) * HEAD_DIM]
        k = (k * _rms_scale(k) * kg_ref[...]).astype(BF16)
        v = kv[:, SWA_KVW + kvh * HEAD_DIM:SWA_KVW + (kvh + 1) * HEAD_DIM].astype(BF16)
        for g in range(group):
            head = kvh * group + g
            qh = q[:, head * HEAD_DIM:(head + 1) * HEAD_DIM]
            qh = (qh * _rms_scale(qh) * qg_ref[...]).astype(BF16)
            s = _dot_nt(qh, k) * scale - slopes[head] * distf
            s = jnp.where(valid, s, -jnp.inf)
            sink = sink_ref[head]
            m = jnp.maximum(jnp.max(s, axis=-1, keepdims=True), sink)
            e = jnp.exp(s - m)
            denom = jnp.sum(e, axis=-1, keepdims=True) + jnp.exp(sink - m)
            outs.append(_dot((e / denom).astype(BF16), v))
    o_ref[0] = jnp.concatenate(outs, axis=-1)


def _swa_attention(proj, q_gain, k_gain, sinks):
    b, t, _ = proj.shape
    w = WINDOW
    q_blk = 3 * SB_W // SWA_QW
    kv_blk = (3 * SB_W + SWA_QW) // (2 * SWA_KVW)
    slopes = tuple(float(2.0 ** (-8.0 * (i + 1) / SWA_HEADS)) for i in range(SWA_HEADS))
    return pl.pallas_call(
        functools.partial(_swa_body, scale=HEAD_DIM ** -0.5, slopes=slopes),
        out_shape=jax.ShapeDtypeStruct((b, t, SWA_QW), F32),
        grid=(b, t // w),
        in_specs=[
            pl.BlockSpec(memory_space=pltpu.SMEM),
            pl.BlockSpec((1, w, SWA_QW), lambda bi, i: (bi, i, q_blk)),
            pl.BlockSpec((1, w, 2 * SWA_KVW), lambda bi, i: (bi, i, kv_blk)),
            pl.BlockSpec((1, w, 2 * SWA_KVW), lambda bi, i: (bi, jnp.maximum(i - 1, 0), kv_blk)),
            pl.BlockSpec((1, HEAD_DIM), lambda bi, i: (0, 0)),
            pl.BlockSpec((1, HEAD_DIM), lambda bi, i: (0, 0)),
        ],
        out_specs=pl.BlockSpec((1, w, SWA_QW), lambda bi, i: (bi, i, 0)),
        compiler_params=_params(("parallel", "parallel")),
        name="swa_attention",
    )(sinks, proj, proj, proj, q_gain.reshape(1, HEAD_DIM), k_gain.reshape(1, HEAD_DIM))


def _gdn_prep_body(x_ref, halo_ref, cw_ref, ba_ref, alog_ref, dtb_ref, o_ref, gb_ref, ext_ref, *, tm):
    ti = pl.program_id(1)
    ci = pl.program_id(2)
    halo = halo_ref[0]
    ext_ref[0:8, :] = jnp.where(ti > 0, halo, jnp.zeros_like(halo))
    ext_ref[8:, :] = x_ref[0]
    y = None
    for tap in range(GDN_CONV):
        off = 8 - (GDN_CONV - 1) + tap
        term = ext_ref[off:off + tm, :] * cw_ref[tap:tap + 1, :]
        y = term if y is None else y + term
    y = y * _sigmoid(y)
    q_blocks = GDN_KW // o_ref.shape[2]
    is_q = ci < q_blocks
    is_k = (ci >= q_blocks) & (ci < 2 * q_blocks)
    for head in range(o_ref.shape[2] // GDN_HEAD_DIM):
        lanes = slice(head * GDN_HEAD_DIM, (head + 1) * GDN_HEAD_DIM)
        yh = y[:, lanes]
        inv = lax.rsqrt(jnp.sum(yh * yh, axis=-1, keepdims=True) + EPS)
        mult = jnp.where(is_q, inv * (GDN_HEAD_DIM ** -0.5), jnp.where(is_k, inv, 1.0))
        o_ref[0, :, lanes] = yh * mult

    @pl.when(ci == 0)
    def _():
        ba = ba_ref[0]
        beta = _sigmoid(ba[:, :GDN_V_HEADS])
        g = -jnp.exp(alog_ref[...]) * _softplus(ba[:, GDN_V_HEADS:2 * GDN_V_HEADS] + dtb_ref[...])
        r = lax.broadcasted_iota(jnp.int32, (GDN_CHUNK, GDN_CHUNK), 0)
        c = lax.broadcasted_iota(jnp.int32, (GDN_CHUNK, GDN_CHUNK), 1)
        lower = jnp.where(r >= c, 1.0, 0.0).astype(F32)
        for chunk in range(tm // GDN_CHUNK):
            rows = slice(chunk * GDN_CHUNK, (chunk + 1) * GDN_CHUNK)
            gc = _dot_f32(lower, g[rows])
            gb_ref[0, rows, :] = jnp.concatenate([gc, beta[rows]], axis=-1)


def _gdn_prep(qkv, ba, conv_w, a_log, dt_bias, tm, cw):
    b, t, c = qkv.shape
    nh = GDN_V_HEADS
    return pl.pallas_call(
        functools.partial(_gdn_prep_body, tm=tm),
        out_shape=[jax.ShapeDtypeStruct((b, t, c), F32), jax.ShapeDtypeStruct((b, t, 2 * nh), F32)],
        grid=(b, t // tm, c // cw),
        in_specs=[
            pl.BlockSpec((1, tm, cw), lambda bi, ti, ci: (bi, ti, ci)),
            pl.BlockSpec((1, 8, cw), lambda bi, ti, ci: (bi, jnp.maximum(ti * (tm // 8) - 1, 0), ci)),
            pl.BlockSpec((GDN_CONV, cw), lambda bi, ti, ci: (0, ci)),
            pl.BlockSpec((1, tm, 2 * nh), lambda bi, ti, ci: (bi, ti, 0)),
            pl.BlockSpec((1, nh), lambda bi, ti, ci: (0, 0)),
            pl.BlockSpec((1, nh), lambda bi, ti, ci: (0, 0)),
        ],
        out_specs=[
            pl.BlockSpec((1, tm, cw), lambda bi, ti, ci: (bi, ti, ci)),
            pl.BlockSpec((1, tm, 2 * nh), lambda bi, ti, ci: (bi, ti, 0)),
        ],
        scratch_shapes=[pltpu.VMEM((tm + 8, cw), F32)],
        compiler_params=_params(("parallel", "parallel", "arbitrary")),
        name="gdn_prep",
    )(qkv, qkv, conv_w, ba, a_log.reshape(1, nh), dt_bias.reshape(1, nh))


def _split(x):
    hi = x.astype(BF16)
    return hi, (x - hi.astype(F32)).astype(BF16)


def _block_diag(y, mask):
    return jnp.where(mask, jnp.concatenate([y, y], axis=0), jnp.zeros((), y.dtype))


def _pair_lhs(x):
    hi, lo = _split(x)
    return jnp.concatenate([hi, hi, lo], axis=1)


def _pair_rhs(y, mask):
    hi, lo = _split(y)
    hi, lo = _block_diag(hi, mask), _block_diag(lo, mask)
    return jnp.concatenate([hi, lo, hi], axis=0)


def _pair_mm(lhs, rhs):
    return _dot(lhs, rhs)


def _unit_lower_inverse_pairs(a_list, eye, same_block, bd_mask):
    mm = lambda xs, ys: [_pair_mm(_pair_lhs(x), _pair_rhs(y, bd_mask)) for x, y in zip(xs, ys)]
    add = lambda xs, ys: [x + y for x, y in zip(xs, ys)]
    c = a_list[0].shape[0]
    ad = [jnp.where(same_block, a, 0.0) for a in a_list]
    lo = [a - d for a, d in zip(a_list, ad)]
    dinv = [eye - d for d in ad]
    power = ad
    span = 2
    while span < INV_BLOCK:
        power = mm(power, power)
        dinv = add(dinv, mm(dinv, power))
        span *= 2
    n = mm(dinv, lo)
    m = [eye - x for x in n]
    power = n
    span = 2
    while span < c // INV_BLOCK:
        power = mm(power, power)
        m = add(m, mm(m, power))
        span *= 2
    return mm(m, dinv)


def _gdn_local_body(q_ref, k_ref, v_ref, gb_ref, gr_ref, u_ref, wq_ref, kt_ref, at_ref, *, n_chunks):
    pair = pl.program_id(1)
    cs = GDN_CHUNK
    nh = GDN_V_HEADS
    d = GDN_HEAD_DIM
    rep = GDN_V_HEADS // GDN_K_HEADS
    assert rep == 2 and rep * cs == LANES

    ii = lax.broadcasted_iota(jnp.int32, (cs, rep * cs), 0)
    lane = lax.broadcasted_iota(jnp.int32, (cs, rep * cs), 1)
    jj = lane % cs
    first = lane < cs
    eye = jnp.where(ii == jj, 1.0, 0.0).astype(F32)
    same_block = (ii // INV_BLOCK) == (jj // INV_BLOCK)
    r2 = lax.broadcasted_iota(jnp.int32, (rep * cs, rep * cs), 0)
    c2 = lax.broadcasted_iota(jnp.int32, (rep * cs, rep * cs), 1)
    bd_mask = (r2 // cs) == (c2 // cs)
    lane_gb = lax.broadcasted_iota(jnp.int32, (cs, 2 * nh), 1)
    zeros_rhs = jnp.zeros((cs, 2 * d), BF16)
    chunks = range(n_chunks)
    rows = [slice(c * cs, (c + 1) * cs) for c in chunks]

    def column(gb, idx):
        return jnp.sum(jnp.where(lane_gb == idx, gb, 0.0), axis=-1, keepdims=True)

    q = [q_ref[0, r, :] for r in rows]
    k = [k_ref[0, r, :] for r in rows]
    gb = [gb_ref[0, r, :] for r in rows]
    gc = [[column(g, rep * pair + h) for h in range(rep)] for g in gb]
    beta = [[column(g, nh + rep * pair + h) for h in range(rep)] for g in gb]
    gc_row = [gr_ref[0, 0, c] for c in chunks]
    decay = [jnp.exp(jnp.where(ii >= jj, jnp.where(first, gc[c][0], gc[c][1]) - gc_row[c], -jnp.inf))
             for c in chunks]
    k16 = [x.astype(BF16) for x in k]
    kk16 = [jnp.concatenate([x, x], axis=0) for x in k16]
    a = [jnp.where(ii > jj, _dot_nt(k16[c], kk16[c]) * jnp.where(first, beta[c][0], beta[c][1]) * decay[c], 0.0)
         for c in chunks]
    for c in chunks:
        at_ref[0, 0, rows[c], :] = jnp.where(
            ii >= jj, _dot_nt(q[c].astype(BF16), kk16[c]) * decay[c], 0.0).astype(BF16)
    tinv = _unit_lower_inverse_pairs(a, eye, same_block, bd_mask)

    eg = [[jnp.exp(gc[c][h]) for h in range(rep)] for c in chunks]
    sol = []
    for c in chunks:
        rhs_hi, rhs_lo = [], []
        for h in range(rep):
            v = v_ref[0, rows[c], h * d:(h + 1) * d]
            hi, lo = _split(jnp.concatenate([v * beta[c][h], k[c] * (beta[c][h] * eg[c][h])], axis=-1))
            pad = [zeros_rhs] * rep
            rhs_hi.append(jnp.concatenate(pad[:h] + [hi] + pad[h + 1:], axis=1))
            rhs_lo.append(jnp.concatenate(pad[:h] + [lo] + pad[h + 1:], axis=1))
        rhs_hi = jnp.concatenate(rhs_hi, axis=0)
        rhs_lo = jnp.concatenate(rhs_lo, axis=0)
        sol.append(_dot(_pair_lhs(tinv[c]), jnp.concatenate([rhs_hi, rhs_lo, rhs_hi], axis=0)))

    for c in chunks:
        for h in range(rep):
            u_ref[0, rows[c], h * d:(h + 1) * d] = sol[c][:, 2 * d * h:2 * d * h + d]
            wq_ref[0, h, c, :cs, :] = sol[c][:, 2 * d * h + d:2 * d * (h + 1)].astype(BF16)
            wq_ref[0, h, c, cs:, :] = (q[c] * eg[c][h]).astype(BF16)
            g_last = gc_row[c][:, h * cs + cs - 1:(h + 1) * cs]
            kt_ref[0, h, c] = (k[c] * jnp.exp(g_last - gc[c][h])).T.astype(BF16)


def _gdn_local(qkv, gb, gc_rows, n_chunks):
    b, t, _ = qkv.shape
    nh = GDN_V_HEADS
    rep = GDN_V_HEADS // GDN_K_HEADS
    cs = GDN_CHUNK
    tt = n_chunks * cs
    d = GDN_HEAD_DIM
    return pl.pallas_call(
        functools.partial(_gdn_local_body, n_chunks=n_chunks),
        out_shape=[
            jax.ShapeDtypeStruct((b, t, GDN_VW), F32),
            jax.ShapeDtypeStruct((b, nh, t // cs, 2 * cs, d), BF16),
            jax.ShapeDtypeStruct((b, nh, t // cs, d, cs), BF16),
            jax.ShapeDtypeStruct((b, GDN_K_HEADS, t, rep * cs), BF16),
        ],
        grid=(b, GDN_K_HEADS, t // tt),
        in_specs=[
            pl.BlockSpec((1, tt, d), lambda bi, p, ti: (bi, ti, p)),
            pl.BlockSpec((1, tt, d), lambda bi, p, ti: (bi, ti, GDN_K_HEADS + p)),
            pl.BlockSpec((1, tt, rep * d), lambda bi, p, ti: (bi, ti, GDN_K_HEADS + p)),
            pl.BlockSpec((1, tt, 2 * nh), lambda bi, p, ti: (bi, ti, 0)),
            pl.BlockSpec((1, 1, n_chunks, 1, rep * cs), lambda bi, p, ti: (bi, p, ti, 0, 0)),
        ],
        out_specs=[
            pl.BlockSpec((1, tt, rep * d), lambda bi, p, ti: (bi, ti, p)),
            pl.BlockSpec((1, rep, n_chunks, 2 * cs, d), lambda bi, p, ti: (bi, p, ti, 0, 0)),
            pl.BlockSpec((1, rep, n_chunks, d, cs), lambda bi, p, ti: (bi, p, ti, 0, 0)),
            pl.BlockSpec((1, 1, tt, rep * cs), lambda bi, p, ti: (bi, p, ti, 0)),
        ],
        compiler_params=_params(("parallel", "parallel", "parallel")),
        name="gdn_local",
    )(qkv, qkv, qkv, gb, gc_rows)


def _gdn_scan_body(u_ref, wq_ref, kt_ref, at_ref, gr_ref, o_ref, s_ref, *, n_chunks, n_pairs):
    cs = GDN_CHUNK
    d = GDN_HEAD_DIM
    rep = GDN_V_HEADS // GDN_K_HEADS
    heads = range(rep * n_pairs)

    @pl.when(pl.program_id(2) == 0)
    def _():
        s_ref[...] = jnp.zeros_like(s_ref)

    zeros_v = jnp.zeros((cs, d), BF16)
    states = [s_ref[i] for i in heads]
    for c in range(n_chunks):
        rows = slice(c * cs, (c + 1) * cs)
        s16 = [s.astype(BF16) for s in states]
        both = [_dot(wq_ref[0, i, c], s16[i]) for i in heads]
        v_new = [(u_ref[0, rows, i * d:(i + 1) * d] - both[i][:cs]).astype(BF16) for i in heads]
        for p in range(n_pairs):
            gc_row = gr_ref[0, p, c]
            diag = []
            for h in range(rep):
                pad = [zeros_v] * rep
                diag.append(jnp.concatenate(pad[:h] + [v_new[rep * p + h]] + pad[h + 1:], axis=1))
            o_ref[0, rows, rep * p * d:rep * (p + 1) * d] = (
                jnp.concatenate([both[rep * p + h][cs:] for h in range(rep)], axis=1)
                + _dot(at_ref[0, p, rows, :], jnp.concatenate(diag, axis=0)))
            for h in range(rep):
                i = rep * p + h
                g_last = gc_row[:, h * cs + cs - 1:(h + 1) * cs]
                states[i] = states[i] * jnp.exp(g_last) + _dot(kt_ref[0, i, c], v_new[i])
    for i in heads:
        s_ref[i] = states[i]


def _gdn_scan(u, wq, kt, attn, gc_rows, n_chunks, n_pairs):
    b, t, _ = u.shape
    rep = GDN_V_HEADS // GDN_K_HEADS
    cs = GDN_CHUNK
    tt = n_chunks * cs
    d = GDN_HEAD_DIM
    nhs = rep * n_pairs
    return pl.pallas_call(
        functools.partial(_gdn_scan_body, n_chunks=n_chunks, n_pairs=n_pairs),
        out_shape=jax.ShapeDtypeStruct((b, t, GDN_VW), F32),
        grid=(b, GDN_K_HEADS // n_pairs, t // tt),
        in_specs=[
            pl.BlockSpec((1, tt, nhs * d), lambda bi, g, ti: (bi, ti, g)),
            pl.BlockSpec((1, nhs, n_chunks, 2 * cs, d), lambda bi, g, ti: (bi, g, ti, 0, 0)),
            pl.BlockSpec((1, nhs, n_chunks, d, cs), lambda bi, g, ti: (bi, g, ti, 0, 0)),
            pl.BlockSpec((1, n_pairs, tt, rep * cs), lambda bi, g, ti: (bi, g, ti, 0)),
            pl.BlockSpec((1, n_pairs, n_chunks, 1, rep * cs), lambda bi, g, ti: (bi, g, ti, 0, 0)),
        ],
        out_specs=pl.BlockSpec((1, tt, nhs * d), lambda bi, g, ti: (bi, ti, g)),
        scratch_shapes=[pltpu.VMEM((nhs, d, d), F32)],
        compiler_params=_params(("parallel", "parallel", "arbitrary")),
        name="gdn_scan",
    )(u, wq, kt, attn, gc_rows)


def _gdn_out_body(h_ref, o_ref_in, z_ref, g_ref, w_ref, out_ref, y_ref):
    for head in range(GDN_V_HEADS):
        lanes = slice(head * GDN_HEAD_DIM, (head + 1) * GDN_HEAD_DIM)
        o = o_ref_in[:, lanes]
        z = z_ref[:, lanes]
        y_ref[:, lanes] = (o * _rms_scale(o) * g_ref[...] * (z * _sigmoid(z))).astype(BF16)
    out_ref[...] = h_ref[...] + _dot(y_ref[...], w_ref[...])


def _gdn_out(h, o, z, gain, w, tm):
    n, d = h.shape
    row = lambda i: (i, 0)
    const = lambda i: (0, 0)
    return pl.pallas_call(
        _gdn_out_body,
        out_shape=jax.ShapeDtypeStruct((n, d), F32),
        grid=(n // tm,),
        in_specs=[
            pl.BlockSpec((tm, d), row),
            pl.BlockSpec((tm, GDN_VW), row),
            pl.BlockSpec((tm, GDN_VW), row),
            pl.BlockSpec((1, GDN_HEAD_DIM), const),
            pl.BlockSpec(w.shape, const, pipeline_mode=pl.Buffered(1)),
        ],
        out_specs=pl.BlockSpec((tm, d), row),
        scratch_shapes=[pltpu.VMEM((tm, GDN_VW), BF16)],
        compiler_params=_params(("parallel",)),
        name="gdn_out",
    )(h, o, z, gain.reshape(1, GDN_HEAD_DIM), w)


def _attention_mixer(h, b, t, gain, w_in, q_gain, k_gain, sinks, w_out, tm):
    d = h.shape[1]
    (proj,) = _norm_proj(h, gain, w_in.astype(BF16), (w_in.shape[1],), tm)
    proj = proj.reshape(b, t, -1)
    a_out = _sb_attention(proj, min(SB_BLOCK, t)).reshape(b * t, SB_W)
    b_out = _swa_attention(proj, q_gain, k_gain, sinks).reshape(b * t, SWA_QW)
    w16 = w_out.astype(BF16)
    return _att_out(h, a_out, b_out, w16[:SB_W], w16[SB_W:], tm)


def _gdn_mixer(h, b, t, gain, w_in, conv_w, a_log, dt_bias, out_gain, w_out, tm):
    nh = GDN_V_HEADS
    ba_w = w_in.shape[1] - GDN_CONV_W - GDN_VW
    pad = (-ba_w) % LANES
    w16 = jnp.pad(w_in.astype(BF16), ((0, 0), (0, pad)))
    qkv, z, ba = _norm_proj(h, gain, w16, (GDN_CONV_W, GDN_VW, ba_w + pad), tm)
    ba = ba[:, :ba_w].reshape(b, t, ba_w)
    qkv, gb = _gdn_prep(qkv.reshape(b, t, GDN_CONV_W), ba, conv_w, a_log, dt_bias,
                        min(tm, t), 1024)
    rep = GDN_V_HEADS // GDN_K_HEADS
    gc_rows = gb[:, :, :nh].reshape(b, t // GDN_CHUNK, GDN_CHUNK, GDN_K_HEADS, rep)
    gc_rows = jnp.transpose(gc_rows, (0, 3, 1, 4, 2)).reshape(
        b, GDN_K_HEADS, t // GDN_CHUNK, 1, rep * GDN_CHUNK)
    u, wq, kt, attn = _gdn_local(qkv, gb, gc_rows, min(GDN_LOCAL_CHUNKS, t // GDN_CHUNK))
    o = _gdn_scan(u, wq, kt, attn, gc_rows, min(GDN_SCAN_CHUNKS, t // GDN_CHUNK), GDN_SCAN_PAIRS)
    return _gdn_out(h, o.reshape(b * t, GDN_VW), z, out_gain, w_out.astype(BF16), tm)


def kernel(x, p, ffn_norm, ffn_w_gate, ffn_w_up, ffn_w_down, mix_norm, att_w_in, att_q_norm, att_k_norm, att_sinks, att_w_out, gdn_w_in, gdn_conv_w, gdn_a_log, gdn_dt_bias, gdn_out_norm, gdn_w_out, ple_norm, ple_w_gate, ple_w_proj):
    b, t, d = x.shape
    n = b * t
    tm = min(TOKEN_TILE, n)
    depth = p.shape[0]
    h = x.reshape(n, d)
    for i in range(depth):
        j = i // 2
        h = _ffn(h, ffn_norm[i, 0], ffn_w_gate[i, 0].astype(BF16), ffn_w_up[i, 0].astype(BF16),
                 ffn_w_down[i, 0].astype(BF16), tm)
        if i % 2 == 0:
            h = _attention_mixer(h, b, t, mix_norm[i], att_w_in[j], att_q_norm[j], att_k_norm[j],
                                 att_sinks[j], att_w_out[j], tm)
        else:
            h = _gdn_mixer(h, b, t, mix_norm[i], gdn_w_in[j], gdn_conv_w[j], gdn_a_log[j],
                           gdn_dt_bias[j], gdn_out_norm[j], gdn_w_out[j], tm)
        h = _ffn(h, ffn_norm[i, 1], ffn_w_gate[i, 1].astype(BF16), ffn_w_up[i, 1].astype(BF16),
                 ffn_w_down[i, 1].astype(BF16), tm)
        h = _ple(h, p[i].reshape(n, -1), ple_norm[i], ple_w_gate[i].astype(BF16),
                 ple_w_proj[i].astype(BF16), tm)
    return h.reshape(b, t, d)
```

```python
import functools

import jax
import jax.numpy as jnp
from jax import lax
from jax.experimental import pallas as pl
from jax.experimental.pallas import tpu as pltpu

F32 = jnp.float32
BF16 = jnp.bfloat16

EPS = 1e-6
LOG2E = 1.4426950408889634
HEAD_DIM = 64
SB_HEADS = 8
SWA_HEADS = 8
SWA_KV_HEADS = 2
WINDOW = 128
GDN_K_HEADS = 8
GDN_V_HEADS = 16
GDN_HEAD_DIM = 128
GDN_CONV = 4
GDN_CHUNK = 64
SB_W = SB_HEADS * HEAD_DIM
SWA_QW = SWA_HEADS * HEAD_DIM
SWA_KVW = SWA_KV_HEADS * HEAD_DIM
GDN_KW = GDN_K_HEADS * GDN_HEAD_DIM
GDN_VW = GDN_V_HEADS * GDN_HEAD_DIM
GDN_CONV_W = 2 * GDN_KW + GDN_VW

LANES = 128
VMEM_LIMIT = 56 * 1024 * 1024

TOKEN_TILE = 512
FF_CHUNK = 256
SB_BLOCK = 256
SB_PAIRS = 4
GDN_LOCAL_CHUNKS = 8
GDN_SCAN_CHUNKS = 4
GDN_SCAN_PAIRS = 4
INV_BLOCK = 16


def _params(sem):
    return pltpu.CompilerParams(dimension_semantics=sem, vmem_limit_bytes=VMEM_LIMIT)


def _dot(a, b):
    return jnp.dot(a, b, preferred_element_type=F32)


def _dot_nt(a, b):
    return lax.dot_general(a, b, (((1,), (1,)), ((), ())), preferred_element_type=F32)


def _dot_f32(a, b):
    return jnp.dot(a, b, preferred_element_type=F32, precision=lax.Precision.HIGHEST)


def _rms_scale(x):
    return lax.rsqrt(jnp.mean(x * x, axis=-1, keepdims=True) + EPS)


def _sigmoid(x):
    return 1.0 / (1.0 + jnp.exp(-x))


def _neg_abs(x):
    bits = lax.bitcast_convert_type(x, jnp.uint32) | jnp.uint32(0x80000000)
    return lax.bitcast_convert_type(bits, F32)


def _softplus(x):
    return jnp.maximum(x, 0.0) + jnp.log1p(jnp.exp(-jnp.abs(x)))


def _ffn_body(x_ref, g_ref, wg_ref, wu_ref, wd_ref, o_ref, acc_ref, *, n_chunks, fc):
    x = x_ref[...]
    xn = (x * _rms_scale(x) * g_ref[...]).astype(BF16)
    for c in range(n_chunks):
        cols = slice(c * fc, (c + 1) * fc)
        gate = _dot(xn, wg_ref[:, cols])
        up = _dot(xn, wu_ref[:, cols])
        act = (gate * _sigmoid(gate) * up).astype(BF16)
        part = _dot(act, wd_ref[cols, :])
        if c == 0:
            acc_ref[...] = part
        else:
            acc_ref[...] += part
    o_ref[...] = x + 0.5 * acc_ref[...]


def _ffn(h, gain, wg, wu, wd, tm):
    n, d = h.shape
    f = wg.shape[1]
    fc = FF_CHUNK if f % FF_CHUNK == 0 else f
    const = lambda i: (0, 0)
    return pl.pallas_call(
        functools.partial(_ffn_body, n_chunks=f // fc, fc=fc),
        out_shape=jax.ShapeDtypeStruct((n, d), F32),
        grid=(n // tm,),
        in_specs=[
            pl.BlockSpec((tm, d), lambda i: (i, 0)),
            pl.BlockSpec((1, d), const),
            pl.BlockSpec((d, f), const, pipeline_mode=pl.Buffered(1)),
            pl.BlockSpec((d, f), const, pipeline_mode=pl.Buffered(1)),
            pl.BlockSpec((f, d), const, pipeline_mode=pl.Buffered(1)),
        ],
        out_specs=pl.BlockSpec((tm, d), lambda i: (i, 0)),
        scratch_shapes=[pltpu.VMEM((tm, d), F32)],
        compiler_params=_params(("parallel",)),
        name="ffn",
    )(h, gain.reshape(1, d), wg, wu, wd)


def _norm_proj_body(x_ref, g_ref, w_ref, *o_refs, outs):
    x = x_ref[...]
    xn = (x * _rms_scale(x) * g_ref[...]).astype(BF16)
    start = 0
    for o_ref, (width, _, scale) in zip(o_refs, outs):
        y = _dot(xn, w_ref[:, start:start + width])
        o_ref[...] = (y if scale is None else y * scale).astype(o_ref.dtype)
        start += width


def _norm_proj(h, gain, w, outs, tm):
    n, d = h.shape
    widths = [o[0] for o in outs]
    assert sum(widths) == w.shape[1]
    return pl.pallas_call(
        functools.partial(_norm_proj_body, outs=outs),
        out_shape=[jax.ShapeDtypeStruct((n, width), dtype) for width, dtype, _ in outs],
        grid=(n // tm,),
        in_specs=[
            pl.BlockSpec((tm, d), lambda i: (i, 0)),
            pl.BlockSpec((1, d), lambda i: (0, 0)),
            pl.BlockSpec(w.shape, lambda i: (0, 0), pipeline_mode=pl.Buffered(1)),
        ],
        out_specs=[pl.BlockSpec((tm, width), lambda i: (i, 0)) for width in widths],
        compiler_params=_params(("parallel",)),
        name="norm_proj",
    )(h, gain.reshape(1, d), w)


def _att_out_body(h_ref, a_ref, b_ref, wa_ref, wb_ref, o_ref):
    o_ref[...] = (h_ref[...] + _dot(a_ref[...].astype(BF16), wa_ref[...])
                  + _dot(b_ref[...].astype(BF16), wb_ref[...]))


def _att_out(h, a, b, wa, wb, tm):
    n, d = h.shape
    row = lambda i: (i, 0)
    const = lambda i: (0, 0)
    return pl.pallas_call(
        _att_out_body,
        out_shape=jax.ShapeDtypeStruct((n, d), F32),
        grid=(n // tm,),
        in_specs=[
            pl.BlockSpec((tm, d), row),
            pl.BlockSpec((tm, a.shape[1]), row),
            pl.BlockSpec((tm, b.shape[1]), row),
            pl.BlockSpec(wa.shape, const, pipeline_mode=pl.Buffered(1)),
            pl.BlockSpec(wb.shape, const, pipeline_mode=pl.Buffered(1)),
        ],
        out_specs=pl.BlockSpec((tm, d), row),
        compiler_params=_params(("parallel",)),
        name="att_out",
    )(h, a, b, wa, wb)


def _ple_body(x_ref, p_ref, g_ref, wg_ref, wp_ref, o_ref):
    x = x_ref[...]
    xn = (x * _rms_scale(x) * g_ref[...]).astype(BF16)
    gate = _sigmoid(_dot(xn, wg_ref[...]))
    o_ref[...] = x + gate * _dot(p_ref[...].astype(BF16), wp_ref[...])


def _ple(h, p, gain, wg, wp, tm):
    n, d = h.shape
    row = lambda i: (i, 0)
    const = lambda i: (0, 0)
    return pl.pallas_call(
        _ple_body,
        out_shape=jax.ShapeDtypeStruct((n, d), F32),
        grid=(n // tm,),
        in_specs=[
            pl.BlockSpec((tm, d), row),
            pl.BlockSpec((tm, p.shape[1]), row),
            pl.BlockSpec((1, d), const),
            pl.BlockSpec(wg.shape, const, pipeline_mode=pl.Buffered(1)),
            pl.BlockSpec(wp.shape, const, pipeline_mode=pl.Buffered(1)),
        ],
        out_specs=pl.BlockSpec((tm, d), row),
        compiler_params=_params(("parallel",)),
        name="ple",
    )(h, p, gain.reshape(1, d), wg, wp)


def _sb_body(q_ref, k_ref, v_ref, o_ref, acc_ref, *, blk, n_pairs):
    qi = pl.program_id(2)
    per = LANES // HEAD_DIM
    assert per == 2
    row = lax.broadcasted_iota(jnp.int32, (blk, blk), 0)
    col = lax.broadcasted_iota(jnp.int32, (blk, blk), 1)
    causal = col < row
    later = jnp.where(row >= col, 1.0, 0.0).astype(BF16)
    later2 = jnp.concatenate([later, later], axis=0)
    first = lax.broadcasted_iota(jnp.int32, (blk, LANES), 1) < HEAD_DIM
    heads = [(p, h) for p in range(n_pairs) for h in range(per)]
    q = []
    for p, h in heads:
        qp = q_ref[0, :, p * LANES:(p + 1) * LANES]
        q.append(jnp.where(first if h == 0 else ~first, qp, jnp.zeros_like(qp)))

    def block(j, carry):
        start = pl.multiple_of(j * blk, blk)
        rows = pl.ds(start, blk)
        n = len(heads)
        z, split, rc, w, pv = [None] * n, [None] * n, [None] * n, [None] * n, [None] * n

        def scores(i):
            p = heads[i][0]
            z[i] = _dot_nt(q[i], k_ref[0, rows, p * LANES:(p + 1) * LANES])

        def keep_terms(i):
            sp = jnp.maximum(z[i], 0.0) + jnp.log(1.0 + jnp.exp2(_neg_abs(z[i]))) * LOG2E
            if carry is None:
                sp = jnp.where(causal, sp, 0.0)
            hi = sp.astype(BF16)
            split[i] = jnp.concatenate([hi, (sp - hi.astype(F32)).astype(BF16)], axis=1)

        def cumulate(i):
            rc[i] = _dot(split[i], later2)

        def weights(i):
            if carry is None:
                w[i] = jnp.where(causal, jnp.exp2(z[i] - rc[i]), 0.0).astype(BF16)
            else:
                w[i] = jnp.exp2(z[i] - rc[i] - carry[i]).astype(BF16)

        def values(i):
            p = heads[i][0]
            pv[i] = _dot(w[i], v_ref[0, rows, p * LANES:(p + 1) * LANES])

        stages = (scores, keep_terms, cumulate, weights, values)
        for tick in range(n + len(stages) - 1):
            for s, stage in enumerate(stages):
                if 0 <= tick - s < n:
                    stage(tick - s)
        return pv, [r[:, 0:1] for r in rc]

    pv, carry = block(qi, None)
    for i, x in enumerate(pv):
        acc_ref[i] = x

    def step(it, carry):
        pv, tot = block(qi - 1 - it, carry)
        for i, x in enumerate(pv):
            acc_ref[i] += x
        return tuple(c + t for c, t in zip(carry, tot))

    lax.fori_loop(0, qi, step, tuple(carry))
    for p in range(n_pairs):
        o_ref[0, :, p * LANES:(p + 1) * LANES] = jnp.where(first, acc_ref[per * p], acc_ref[per * p + 1])


def _sb_attention(q, kv, blk, n_pairs):
    b, t, _ = q.shape
    width = n_pairs * LANES
    groups = SB_W // width
    return pl.pallas_call(
        functools.partial(_sb_body, blk=blk, n_pairs=n_pairs),
        out_shape=jax.ShapeDtypeStruct((b, t, SB_W), F32),
        grid=(b, groups, t // blk),
        in_specs=[
            pl.BlockSpec((1, blk, width), lambda bi, g, qi: (bi, qi, g)),
            pl.BlockSpec((1, t, width), lambda bi, g, qi: (bi, 0, g)),
            pl.BlockSpec((1, t, width), lambda bi, g, qi: (bi, 0, groups + g)),
        ],
        out_specs=pl.BlockSpec((1, blk, width), lambda bi, g, qi: (bi, qi, g)),
        scratch_shapes=[pltpu.VMEM((2 * n_pairs, blk, LANES), F32)],
        compiler_params=_params(("parallel", "parallel", "arbitrary")),
        name="sb_attention",
    )(q, kv, kv)


def _swa_body(sink_ref, q_ref, kvc_ref, kvp_ref, qg_ref, kg_ref, o_ref, *, scale, slopes):
    blk = pl.program_id(1)
    w = WINDOW
    q = q_ref[0]
    kv = jnp.concatenate([kvp_ref[0], kvc_ref[0]], axis=0)
    qpos = lax.broadcasted_iota(jnp.int32, (w, 2 * w), 0)
    kpos = lax.broadcasted_iota(jnp.int32, (w, 2 * w), 1)
    dist = qpos + w - kpos
    valid = (dist >= 0) & (dist < w) & ((blk > 0) | (kpos >= w))
    distf = dist.astype(F32)
    group = SWA_HEADS // SWA_KV_HEADS
    outs = []
    for kvh in range(SWA_KV_HEADS):
        k = kv[:, kvh * HEAD_DIM:(kvh + 1) * HEAD_DIM]
        k = (k * _rms_scale(k) * kg_ref[...]).astype(BF16)
        v = kv[:, SWA_KVW + kvh * HEAD_DIM:SWA_KVW + (kvh + 1) * HEAD_DIM].astype(BF16)
        for g in range(group):
            head = kvh * group + g
            qh = q[:, head * HEAD_DIM:(head + 1) * HEAD_DIM]
            qh = (qh * _rms_scale(qh) * qg_ref[...]).astype(BF16)
            s = _dot_nt(qh, k) * scale - slopes[head] * distf
            s = jnp.where(valid, s, -jnp.inf)
            sink = sink_ref[head]
            m = jnp.maximum(jnp.max(s, axis=-1, keepdims=True), sink)
            e = jnp.exp(s - m)
            denom = jnp.sum(e, axis=-1, keepdims=True) + jnp.exp(sink - m)
            outs.append(_dot((e / denom).astype(BF16), v))
    o_ref[0] = jnp.concatenate(outs, axis=-1)


def _swa_attention(proj, q_gain, k_gain, sinks):
    b, t, _ = proj.shape
    w = WINDOW
    q_blk = 0
    kv_blk = SWA_QW // (2 * SWA_KVW)
    slopes = tuple(float(2.0 ** (-8.0 * (i + 1) / SWA_HEADS)) for i in range(SWA_HEADS))
    return pl.pallas_call(
        functools.partial(_swa_body, scale=HEAD_DIM ** -0.5, slopes=slopes),
        out_shape=jax.ShapeDtypeStruct((b, t, SWA_QW), F32),
        grid=(b, t // w),
        in_specs=[
            pl.BlockSpec(memory_space=pltpu.SMEM),
            pl.BlockSpec((1, w, SWA_QW), lambda bi, i: (bi, i, q_blk)),
            pl.BlockSpec((1, w, 2 * SWA_KVW), lambda bi, i: (bi, i, kv_blk)),
            pl.BlockSpec((1, w, 2 * SWA_KVW), lambda bi, i: (bi, jnp.maximum(i - 1, 0), kv_blk)),
            pl.BlockSpec((1, HEAD_DIM), lambda bi, i: (0, 0)),
            pl.BlockSpec((1, HEAD_DIM), lambda bi, i: (0, 0)),
        ],
        out_specs=pl.BlockSpec((1, w, SWA_QW), lambda bi, i: (bi, i, 0)),
        compiler_params=_params(("parallel", "parallel")),
        name="swa_attention",
    )(sinks, proj, proj, proj, q_gain.reshape(1, HEAD_DIM), k_gain.reshape(1, HEAD_DIM))


def _gdn_prep_body(x_ref, halo_ref, cw_ref, ba_ref, alog_ref, dtb_ref, o_ref, gb_ref, ext_ref, *, tm):
    ti = pl.program_id(1)
    ci = pl.program_id(2)
    halo = halo_ref[0]
    ext_ref[0:8, :] = jnp.where(ti > 0, halo, jnp.zeros_like(halo))
    ext_ref[8:, :] = x_ref[0]
    y = None
    for tap in range(GDN_CONV):
        off = 8 - (GDN_CONV - 1) + tap
        term = ext_ref[off:off + tm, :] * cw_ref[tap:tap + 1, :]
        y = term if y is None else y + term
    y = y * _sigmoid(y)
    q_blocks = GDN_KW // o_ref.shape[2]
    is_q = ci < q_blocks
    is_k = (ci >= q_blocks) & (ci < 2 * q_blocks)
    for head in range(o_ref.shape[2] // GDN_HEAD_DIM):
        lanes = slice(head * GDN_HEAD_DIM, (head + 1) * GDN_HEAD_DIM)
        yh = y[:, lanes]
        inv = lax.rsqrt(jnp.sum(yh * yh, axis=-1, keepdims=True) + EPS)
        mult = jnp.where(is_q, inv * (GDN_HEAD_DIM ** -0.5), jnp.where(is_k, inv, 1.0))
        o_ref[0, :, lanes] = yh * mult

    @pl.when(ci == 0)
    def _():
        ba = ba_ref[0]
        beta = _sigmoid(ba[:, :GDN_V_HEADS])
        g = -jnp.exp(alog_ref[...]) * _softplus(ba[:, GDN_V_HEADS:2 * GDN_V_HEADS] + dtb_ref[...])
        r = lax.broadcasted_iota(jnp.int32, (GDN_CHUNK, GDN_CHUNK), 0)
        c = lax.broadcasted_iota(jnp.int32, (GDN_CHUNK, GDN_CHUNK), 1)
        lower = jnp.where(r >= c, 1.0, 0.0).astype(F32)
        for chunk in range(tm // GDN_CHUNK):
            rows = slice(chunk * GDN_CHUNK, (chunk + 1) * GDN_CHUNK)
            gc = _dot_f32(lower, g[rows])
            gb_ref[0, rows, :] = jnp.concatenate([gc, beta[rows]], axis=-1)


def _gdn_prep(qkv, ba, conv_w, a_log, dt_bias, tm, cw):
    b, t, c = qkv.shape
    nh = GDN_V_HEADS
    return pl.pallas_call(
        functools.partial(_gdn_prep_body, tm=tm),
        out_shape=[jax.ShapeDtypeStruct((b, t, c), F32), jax.ShapeDtypeStruct((b, t, 2 * nh), F32)],
        grid=(b, t // tm, c // cw),
        in_specs=[
            pl.BlockSpec((1, tm, cw), lambda bi, ti, ci: (bi, ti, ci)),
            pl.BlockSpec((1, 8, cw), lambda bi, ti, ci: (bi, jnp.maximum(ti * (tm // 8) - 1, 0), ci)),
            pl.BlockSpec((GDN_CONV, cw), lambda bi, ti, ci: (0, ci)),
            pl.BlockSpec((1, tm, 2 * nh), lambda bi, ti, ci: (bi, ti, 0)),
            pl.BlockSpec((1, nh), lambda bi, ti, ci: (0, 0)),
            pl.BlockSpec((1, nh), lambda bi, ti, ci: (0, 0)),
        ],
        out_specs=[
            pl.BlockSpec((1, tm, cw), lambda bi, ti, ci: (bi, ti, ci)),
            pl.BlockSpec((1, tm, 2 * nh), lambda bi, ti, ci: (bi, ti, 0)),
        ],
        scratch_shapes=[pltpu.VMEM((tm + 8, cw), F32)],
        compiler_params=_params(("parallel", "parallel", "arbitrary")),
        name="gdn_prep",
    )(qkv, qkv, conv_w, ba, a_log.reshape(1, nh), dt_bias.reshape(1, nh))


def _split(x):
    hi = x.astype(BF16)
    return hi, (x - hi.astype(F32)).astype(BF16)


def _block_diag(y, mask):
    return jnp.where(mask, jnp.concatenate([y, y], axis=0), jnp.zeros((), y.dtype))


def _pair_lhs(x):
    hi, lo = _split(x)
    return jnp.concatenate([hi, hi, lo], axis=1)


def _pair_rhs(y, mask):
    hi, lo = _split(y)
    hi, lo = _block_diag(hi, mask), _block_diag(lo, mask)
    return jnp.concatenate([hi, lo, hi], axis=0)


def _pair_mm(lhs, rhs):
    return _dot(lhs, rhs)


def _unit_lower_inverse_pairs(a_list, eye, same_block, bd_mask):
    mm = lambda xs, ys: [_pair_mm(_pair_lhs(x), _pair_rhs(y, bd_mask)) for x, y in zip(xs, ys)]
    add = lambda xs, ys: [x + y for x, y in zip(xs, ys)]
    c = a_list[0].shape[0]
    ad = [jnp.where(same_block, a, 0.0) for a in a_list]
    lo = [a - d for a, d in zip(a_list, ad)]
    dinv = [eye - d for d in ad]
    power = ad
    span = 2
    while span < INV_BLOCK:
        power = mm(power, power)
        dinv = add(dinv, mm(dinv, power))
        span *= 2
    n = mm(dinv, lo)
    m = [eye - x for x in n]
    power = n
    span = 2
    while span < c // INV_BLOCK:
        power = mm(power, power)
        m = add(m, mm(m, power))
        span *= 2
    return mm(m, dinv)


def _gdn_local_body(q_ref, k_ref, v_ref, gb_ref, gr_ref, u_ref, wq_ref, kt_ref, at_ref, *, n_chunks):
    pair = pl.program_id(1)
    cs = GDN_CHUNK
    nh = GDN_V_HEADS
    d = GDN_HEAD_DIM
    rep = GDN_V_HEADS // GDN_K_HEADS
    assert rep == 2 and rep * cs == LANES

    ii = lax.broadcasted_iota(jnp.int32, (cs, rep * cs), 0)
    lane = lax.broadcasted_iota(jnp.int32, (cs, rep * cs), 1)
    jj = lane % cs
    first = lane < cs
    eye = jnp.where(ii == jj, 1.0, 0.0).astype(F32)
    same_block = (ii // INV_BLOCK) == (jj // INV_BLOCK)
    r2 = lax.broadcasted_iota(jnp.int32, (rep * cs, rep * cs), 0)
    c2 = lax.broadcasted_iota(jnp.int32, (rep * cs, rep * cs), 1)
    bd_mask = (r2 // cs) == (c2 // cs)
    lane_gb = lax.broadcasted_iota(jnp.int32, (cs, 2 * nh), 1)
    zeros_rhs = jnp.zeros((cs, 2 * d), BF16)
    chunks = range(n_chunks)
    rows = [slice(c * cs, (c + 1) * cs) for c in chunks]

    def column(gb, idx):
        return jnp.sum(jnp.where(lane_gb == idx, gb, 0.0), axis=-1, keepdims=True)

    q = [q_ref[0, r, :] for r in rows]
    k = [k_ref[0, r, :] for r in rows]
    gb = [gb_ref[0, r, :] for r in rows]
    gc = [[column(g, rep * pair + h) for h in range(rep)] for g in gb]
    beta = [[column(g, nh + rep * pair + h) for h in range(rep)] for g in gb]
    gc_row = [gr_ref[0, 0, c] for c in chunks]
    decay = [jnp.exp(jnp.where(ii >= jj, jnp.where(first, gc[c][0], gc[c][1]) - gc_row[c], -jnp.inf))
             for c in chunks]
    k16 = [x.astype(BF16) for x in k]
    kk16 = [jnp.concatenate([x, x], axis=0) for x in k16]
    a = [jnp.where(ii > jj, _dot_nt(k16[c], kk16[c]) * jnp.where(first, beta[c][0], beta[c][1]) * decay[c], 0.0)
         for c in chunks]
    for c in chunks:
        at_ref[0, 0, rows[c], :] = jnp.where(
            ii >= jj, _dot_nt(q[c].astype(BF16), kk16[c]) * decay[c], 0.0).astype(BF16)
    tinv = _unit_lower_inverse_pairs(a, eye, same_block, bd_mask)

    eg = [[jnp.exp(gc[c][h]) for h in range(rep)] for c in chunks]
    sol = []
    for c in chunks:
        rhs_hi, rhs_lo = [], []
        for h in range(rep):
            v = v_ref[0, rows[c], h * d:(h + 1) * d]
            hi, lo = _split(jnp.concatenate([v * beta[c][h], k[c] * (beta[c][h] * eg[c][h])], axis=-1))
            pad = [zeros_rhs] * rep
            rhs_hi.append(jnp.concatenate(pad[:h] + [hi] + pad[h + 1:], axis=1))
            rhs_lo.append(jnp.concatenate(pad[:h] + [lo] + pad[h + 1:], axis=1))
        rhs_hi = jnp.concatenate(rhs_hi, axis=0)
        rhs_lo = jnp.concatenate(rhs_lo, axis=0)
        sol.append(_dot(_pair_lhs(tinv[c]), jnp.concatenate([rhs_hi, rhs_lo, rhs_hi], axis=0)))

    for c in chunks:
        for h in range(rep):
            u_ref[0, rows[c], h * d:(h + 1) * d] = sol[c][:, 2 * d * h:2 * d * h + d]
            wq_ref[0, h, c, :cs, :] = sol[c][:, 2 * d * h + d:2 * d * (h + 1)].astype(BF16)
            wq_ref[0, h, c, cs:, :] = (q[c] * eg[c][h]).astype(BF16)
            g_last = gc_row[c][:, h * cs + cs - 1:(h + 1) * cs]
            kt_ref[0, h, c] = (k[c] * jnp.exp(g_last - gc[c][h])).T.astype(BF16)


def _gdn_local(qkv, gb, gc_rows, n_chunks):
    b, t, _ = qkv.shape
    nh = GDN_V_HEADS
    rep = GDN_V_HEADS // GDN_K_HEADS
    cs = GDN_CHUNK
    tt = n_chunks * cs
    d = GDN_HEAD_DIM
    return pl.pallas_call(
        functools.partial(_gdn_local_body, n_chunks=n_chunks),
        out_shape=[
            jax.ShapeDtypeStruct((b, t, GDN_VW), F32),
            jax.ShapeDtypeStruct((b, nh, t // cs, 2 * cs, d), BF16),
            jax.ShapeDtypeStruct((b, nh, t // cs, d, cs), BF16),
            jax.ShapeDtypeStruct((b, GDN_K_HEADS, t, rep * cs), BF16),
        ],
        grid=(b, GDN_K_HEADS, t // tt),
        in_specs=[
            pl.BlockSpec((1, tt, d), lambda bi, p, ti: (bi, ti, p)),
            pl.BlockSpec((1, tt, d), lambda bi, p, ti: (bi, ti, GDN_K_HEADS + p)),
            pl.BlockSpec((1, tt, rep * d), lambda bi, p, ti: (bi, ti, GDN_K_HEADS + p)),
            pl.BlockSpec((1, tt, 2 * nh), lambda bi, p, ti: (bi, ti, 0)),
            pl.BlockSpec((1, 1, n_chunks, 1, rep * cs), lambda bi, p, ti: (bi, p, ti, 0, 0)),
        ],
        out_specs=[
            pl.BlockSpec((1, tt, rep * d), lambda bi, p, ti: (bi, ti, p)),
            pl.BlockSpec((1, rep, n_chunks, 2 * cs, d), lambda bi, p, ti: (bi, p, ti, 0, 0)),
            pl.BlockSpec((1, rep, n_chunks, d, cs), lambda bi, p, ti: (bi, p, ti, 0, 0)),
            pl.BlockSpec((1, 1, tt, rep * cs), lambda bi, p, ti: (bi, p, ti, 0)),
        ],
        compiler_params=_params(("parallel", "parallel", "parallel")),
        name="gdn_local",
    )(qkv, qkv, qkv, gb, gc_rows)


def _gdn_scan_body(u_ref, wq_ref, kt_ref, at_ref, gr_ref, o_ref, s_ref, *, n_chunks, n_pairs):
    cs = GDN_CHUNK
    d = GDN_HEAD_DIM
    rep = GDN_V_HEADS // GDN_K_HEADS
    heads = range(rep * n_pairs)

    @pl.when(pl.program_id(2) == 0)
    def _():
        s_ref[...] = jnp.zeros_like(s_ref)

    zeros_v = jnp.zeros((cs, d), BF16)
    states = [s_ref[i] for i in heads]
    for c in range(n_chunks):
        rows = slice(c * cs, (c + 1) * cs)
        s16 = [s.astype(BF16) for s in states]
        both = [_dot(wq_ref[0, i, c], s16[i]) for i in heads]
        v_new = [(u_ref[0, rows, i * d:(i + 1) * d] - both[i][:cs]).astype(BF16) for i in heads]
        for p in range(n_pairs):
            gc_row = gr_ref[0, p, c]
            diag = []
            for h in range(rep):
                pad = [zeros_v] * rep
                diag.append(jnp.concatenate(pad[:h] + [v_new[rep * p + h]] + pad[h + 1:], axis=1))
            o_ref[0, rows, rep * p * d:rep * (p + 1) * d] = (
                jnp.concatenate([both[rep * p + h][cs:] for h in range(rep)], axis=1)
                + _dot(at_ref[0, p, rows, :], jnp.concatenate(diag, axis=0)))
            for h in range(rep):
                i = rep * p + h
                g_last = gc_row[:, h * cs + cs - 1:(h + 1) * cs]
                states[i] = states[i] * jnp.exp(g_last) + _dot(kt_ref[0, i, c], v_new[i])
    for i in heads:
        s_ref[i] = states[i]


def _gdn_scan(u, wq, kt, attn, gc_rows, n_chunks, n_pairs):
    b, t, _ = u.shape
    rep = GDN_V_HEADS // GDN_K_HEADS
    cs = GDN_CHUNK
    tt = n_chunks * cs
    d = GDN_HEAD_DIM
    nhs = rep * n_pairs
    return pl.pallas_call(
        functools.partial(_gdn_scan_body, n_chunks=n_chunks, n_pairs=n_pairs),
        out_shape=jax.ShapeDtypeStruct((b, t, GDN_VW), F32),
        grid=(b, GDN_K_HEADS // n_pairs, t // tt),
        in_specs=[
            pl.BlockSpec((1, tt, nhs * d), lambda bi, g, ti: (bi, ti, g)),
            pl.BlockSpec((1, nhs, n_chunks, 2 * cs, d), lambda bi, g, ti: (bi, g, ti, 0, 0)),
            pl.BlockSpec((1, nhs, n_chunks, d, cs), lambda bi, g, ti: (bi, g, ti, 0, 0)),
            pl.BlockSpec((1, n_pairs, tt, rep * cs), lambda bi, g, ti: (bi, g, ti, 0)),
            pl.BlockSpec((1, n_pairs, n_chunks, 1, rep * cs), lambda bi, g, ti: (bi, g, ti, 0, 0)),
        ],
        out_specs=pl.BlockSpec((1, tt, nhs * d), lambda bi, g, ti: (bi, ti, g)),
        scratch_shapes=[pltpu.VMEM((nhs, d, d), F32)],
        compiler_params=_params(("parallel", "parallel", "arbitrary")),
        name="gdn_scan",
    )(u, wq, kt, attn, gc_rows)


def _gdn_out_body(h_ref, o_ref_in, z_ref, g_ref, w_ref, out_ref, y_ref):
    for head in range(GDN_V_HEADS):
        lanes = slice(head * GDN_HEAD_DIM, (head + 1) * GDN_HEAD_DIM)
        o = o_ref_in[:, lanes]
        z = z_ref[:, lanes]
        y_ref[:, lanes] = (o * _rms_scale(o) * g_ref[...] * (z * _sigmoid(z))).astype(BF16)
    out_ref[...] = h_ref[...] + _dot(y_ref[...], w_ref[...])


def _gdn_out(h, o, z, gain, w, tm):
    n, d = h.shape
    row = lambda i: (i, 0)
    const = lambda i: (0, 0)
    return pl.pallas_call(
        _gdn_out_body,
        out_shape=jax.ShapeDtypeStruct((n, d), F32),
        grid=(n // tm,),
        in_specs=[
            pl.BlockSpec((tm, d), row),
            pl.BlockSpec((tm, GDN_VW), row),
            pl.BlockSpec((tm, GDN_VW), row),
            pl.BlockSpec((1, GDN_HEAD_DIM), const),
            pl.BlockSpec(w.shape, const, pipeline_mode=pl.Buffered(1)),
        ],
        out_specs=pl.BlockSpec((tm, d), row),
        scratch_shapes=[pltpu.VMEM((tm, GDN_VW), BF16)],
        compiler_params=_params(("parallel",)),
        name="gdn_out",
    )(h, o, z, gain.reshape(1, GDN_HEAD_DIM), w)


def _attention_mixer(h, b, t, gain, w_in, q_gain, k_gain, sinks, w_out, tm):
    d = h.shape[1]
    swa_w = SWA_QW + 2 * SWA_KVW
    outs = ((SB_W, BF16, HEAD_DIM ** -0.5 * LOG2E), (2 * SB_W, BF16, None), (swa_w, F32, None))
    sb_q, sb_kv, swa = _norm_proj(h, gain, w_in.astype(BF16), outs, tm)
    a_out = _sb_attention(sb_q.reshape(b, t, SB_W), sb_kv.reshape(b, t, 2 * SB_W),
                          min(SB_BLOCK, t), SB_PAIRS).reshape(b * t, SB_W)
    b_out = _swa_attention(swa.reshape(b, t, swa_w), q_gain, k_gain, sinks).reshape(b * t, SWA_QW)
    w16 = w_out.astype(BF16)
    return _att_out(h, a_out, b_out, w16[:SB_W], w16[SB_W:], tm)


def _gdn_mixer(h, b, t, gain, w_in, conv_w, a_log, dt_bias, out_gain, w_out, tm):
    nh = GDN_V_HEADS
    ba_w = w_in.shape[1] - GDN_CONV_W - GDN_VW
    pad = (-ba_w) % LANES
    w16 = jnp.pad(w_in.astype(BF16), ((0, 0), (0, pad)))
    outs = ((GDN_CONV_W, F32, None), (GDN_VW, F32, None), (ba_w + pad, F32, None))
    qkv, z, ba = _norm_proj(h, gain, w16, outs, tm)
    ba = ba[:, :ba_w].reshape(b, t, ba_w)
    qkv, gb = _gdn_prep(qkv.reshape(b, t, GDN_CONV_W), ba, conv_w, a_log, dt_bias,
                        min(tm, t), 1024)
    rep = GDN_V_HEADS // GDN_K_HEADS
    gc_rows = gb[:, :, :nh].reshape(b, t // GDN_CHUNK, GDN_CHUNK, GDN_K_HEADS, rep)
    gc_rows = jnp.transpose(gc_rows, (0, 3, 1, 4, 2)).reshape(
        b, GDN_K_HEADS, t // GDN_CHUNK, 1, rep * GDN_CHUNK)
    u, wq, kt, attn = _gdn_local(qkv, gb, gc_rows, min(GDN_LOCAL_CHUNKS, t // GDN_CHUNK))
    o = _gdn_scan(u, wq, kt, attn, gc_rows, min(GDN_SCAN_CHUNKS, t // GDN_CHUNK), GDN_SCAN_PAIRS)
    return _gdn_out(h, o.reshape(b * t, GDN_VW), z, out_gain, w_out.astype(BF16), tm)


def kernel(x, p, ffn_norm, ffn_w_gate, ffn_w_up, ffn_w_down, mix_norm, att_w_in, att_q_norm, att_k_norm, att_sinks, att_w_out, gdn_w_in, gdn_conv_w, gdn_a_log, gdn_dt_bias, gdn_out_norm, gdn_w_out, ple_norm, ple_w_gate, ple_w_proj):
    b, t, d = x.shape
    n = b * t
    tm = min(TOKEN_TILE, n)
    depth = p.shape[0]
    h = x.reshape(n, d)
    for i in range(depth):
        j = i // 2
        h = _ffn(h, ffn_norm[i, 0], ffn_w_gate[i, 0].astype(BF16), ffn_w_up[i, 0].astype(BF16),
                 ffn_w_down[i, 0].astype(BF16), tm)
        if i % 2 == 0:
            h = _attention_mixer(h, b, t, mix_norm[i], att_w_in[j], att_q_norm[j], att_k_norm[j],
                                 att_sinks[j], att_w_out[j], tm)
        else:
            h = _gdn_mixer(h, b, t, mix_norm[i], gdn_w_in[j], gdn_conv_w[j], gdn_a_log[j],
                           gdn_dt_bias[j], gdn_out_norm[j], gdn_w_out[j], tm)
        h = _ffn(h, ffn_norm[i, 1], ffn_w_gate[i, 1].astype(BF16), ffn_w_up[i, 1].astype(BF16),
                 ffn_w_down[i, 1].astype(BF16), tm)
        h = _ple(h, p[i].reshape(n, -1), ple_norm[i], ple_w_gate[i].astype(BF16),
                 ple_w_proj[i].astype(BF16), tm)
    return h.reshape(b, t, d)
```

```python
import functools

import jax
import jax.numpy as jnp
from jax import lax
from jax.experimental import pallas as pl
from jax.experimental.pallas import tpu as pltpu

F32 = jnp.float32
BF16 = jnp.bfloat16

EPS = 1e-6
LOG2E = 1.4426950408889634
HEAD_DIM = 64
SB_HEADS = 8
SWA_HEADS = 8
SWA_KV_HEADS = 2
WINDOW = 128
GDN_K_HEADS = 8
GDN_V_HEADS = 16
GDN_HEAD_DIM = 128
GDN_CONV = 4
GDN_CHUNK = 64
SB_W = SB_HEADS * HEAD_DIM
SWA_QW = SWA_HEADS * HEAD_DIM
SWA_KVW = SWA_KV_HEADS * HEAD_DIM
GDN_KW = GDN_K_HEADS * GDN_HEAD_DIM
GDN_VW = GDN_V_HEADS * GDN_HEAD_DIM
GDN_CONV_W = 2 * GDN_KW + GDN_VW

LANES = 128
VMEM_LIMIT = 56 * 1024 * 1024

TOKEN_TILE = 512
FF_CHUNK = 256
SB_BLOCK = 256
SB_PAIRS = 4
SWA_TILE_BLOCKS = 4
GDN_IN_COLS = 1024
GDN_IN_SUB = 256
CONV_HALO = 16
GDN_LOCAL_CHUNKS = 8
GDN_SCAN_CHUNKS = 4
GDN_SCAN_PAIRS = 4
INV_BLOCK = 16


def _params(sem):
    return pltpu.CompilerParams(dimension_semantics=sem, vmem_limit_bytes=VMEM_LIMIT)


def _dot(a, b):
    return jnp.dot(a, b, preferred_element_type=F32)


def _dot_nt(a, b):
    return lax.dot_general(a, b, (((1,), (1,)), ((), ())), preferred_element_type=F32)


def _dot_f32(a, b):
    return jnp.dot(a, b, preferred_element_type=F32, precision=lax.Precision.HIGHEST)


def _rms_scale(x):
    return lax.rsqrt(jnp.mean(x * x, axis=-1, keepdims=True) + EPS)


def _sigmoid(x):
    return 1.0 / (1.0 + jnp.exp(-x))


def _emit_skewed(n, stages):
    for tick in range(n + len(stages) - 1):
        for s, stage in enumerate(stages):
            if 0 <= tick - s < n:
                stage(tick - s)


def _neg_abs(x):
    bits = lax.bitcast_convert_type(x, jnp.uint32) | jnp.uint32(0x80000000)
    return lax.bitcast_convert_type(bits, F32)


def _softplus(x):
    return jnp.maximum(x, 0.0) + jnp.log1p(jnp.exp(-jnp.abs(x)))


def _ffn_body(x_ref, g_ref, wg_ref, wu_ref, wd_ref, o_ref, acc_ref, *, n_chunks, fc):
    x = x_ref[...]
    xn = (x * _rms_scale(x) * g_ref[...]).astype(BF16)
    for c in range(n_chunks):
        cols = slice(c * fc, (c + 1) * fc)
        gate = _dot(xn, wg_ref[:, cols])
        up = _dot(xn, wu_ref[:, cols])
        act = (gate * _sigmoid(gate) * up).astype(BF16)
        part = _dot(act, wd_ref[cols, :])
        if c == 0:
            acc_ref[...] = part
        else:
            acc_ref[...] += part
    o_ref[...] = x + 0.5 * acc_ref[...]


def _ffn(h, gain, wg, wu, wd, tm):
    n, d = h.shape
    f = wg.shape[1]
    fc = FF_CHUNK if f % FF_CHUNK == 0 else f
    const = lambda i: (0, 0)
    return pl.pallas_call(
        functools.partial(_ffn_body, n_chunks=f // fc, fc=fc),
        out_shape=jax.ShapeDtypeStruct((n, d), F32),
        grid=(n // tm,),
        in_specs=[
            pl.BlockSpec((tm, d), lambda i: (i, 0)),
            pl.BlockSpec((1, d), const),
            pl.BlockSpec((d, f), const, pipeline_mode=pl.Buffered(1)),
            pl.BlockSpec((d, f), const, pipeline_mode=pl.Buffered(1)),
            pl.BlockSpec((f, d), const, pipeline_mode=pl.Buffered(1)),
        ],
        out_specs=pl.BlockSpec((tm, d), lambda i: (i, 0)),
        scratch_shapes=[pltpu.VMEM((tm, d), F32)],
        compiler_params=_params(("parallel",)),
        name="ffn",
    )(h, gain.reshape(1, d), wg, wu, wd)


def _norm_proj_body(x_ref, g_ref, w_ref, *o_refs, outs):
    x = x_ref[...]
    xn = (x * _rms_scale(x) * g_ref[...]).astype(BF16)
    start = 0
    for o_ref, (width, _, scale) in zip(o_refs, outs):
        y = _dot(xn, w_ref[:, start:start + width])
        o_ref[...] = (y if scale is None else y * scale).astype(o_ref.dtype)
        start += width


def _norm_proj(h, gain, w, outs, tm):
    n, d = h.shape
    widths = [o[0] for o in outs]
    assert sum(widths) == w.shape[1]
    return pl.pallas_call(
        functools.partial(_norm_proj_body, outs=outs),
        out_shape=[jax.ShapeDtypeStruct((n, width), dtype) for width, dtype, _ in outs],
        grid=(n // tm,),
        in_specs=[
            pl.BlockSpec((tm, d), lambda i: (i, 0)),
            pl.BlockSpec((1, d), lambda i: (0, 0)),
            pl.BlockSpec(w.shape, lambda i: (0, 0), pipeline_mode=pl.Buffered(1)),
        ],
        out_specs=[pl.BlockSpec((tm, width), lambda i: (i, 0)) for width in widths],
        compiler_params=_params(("parallel",)),
        name="norm_proj",
    )(h, gain.reshape(1, d), w)


def _att_out_body(h_ref, a_ref, b_ref, w_ref, o_ref):
    wa = a_ref.shape[1]
    o_ref[...] = (h_ref[...] + _dot(a_ref[...].astype(BF16), w_ref[:wa, :])
                  + _dot(b_ref[...].astype(BF16), w_ref[wa:, :]))


def _att_out(h, a, b, w, tm):
    n, d = h.shape
    row = lambda i: (i, 0)
    const = lambda i: (0, 0)
    assert a.shape[1] + b.shape[1] == w.shape[0]
    return pl.pallas_call(
        _att_out_body,
        out_shape=jax.ShapeDtypeStruct((n, d), F32),
        grid=(n // tm,),
        in_specs=[
            pl.BlockSpec((tm, d), row),
            pl.BlockSpec((tm, a.shape[1]), row),
            pl.BlockSpec((tm, b.shape[1]), row),
            pl.BlockSpec(w.shape, const, pipeline_mode=pl.Buffered(1)),
        ],
        out_specs=pl.BlockSpec((tm, d), row),
        compiler_params=_params(("parallel",)),
        name="att_out",
    )(h, a, b, w)


def _ple_body(x_ref, p_ref, g_ref, wg_ref, wp_ref, o_ref):
    x = x_ref[...]
    xn = (x * _rms_scale(x) * g_ref[...]).astype(BF16)
    gate = _sigmoid(_dot(xn, wg_ref[...]))
    o_ref[...] = x + gate * _dot(p_ref[0].astype(BF16), wp_ref[...])


def _ple(h, p, layer, gain, wg, wp, tm):
    n, d = h.shape
    row = lambda i: (i, 0)
    const = lambda i: (0, 0)
    return pl.pallas_call(
        _ple_body,
        out_shape=jax.ShapeDtypeStruct((n, d), F32),
        grid=(n // tm,),
        in_specs=[
            pl.BlockSpec((tm, d), row),
            pl.BlockSpec((1, tm, p.shape[2]), lambda i: (layer, i, 0)),
            pl.BlockSpec((1, d), const),
            pl.BlockSpec(wg.shape, const, pipeline_mode=pl.Buffered(1)),
            pl.BlockSpec(wp.shape, const, pipeline_mode=pl.Buffered(1)),
        ],
        out_specs=pl.BlockSpec((tm, d), row),
        compiler_params=_params(("parallel",)),
        name="ple",
    )(h, p, gain.reshape(1, d), wg, wp)


def _sb_body(q_ref, k_ref, v_ref, o_ref, acc_ref, *, blk, n_pairs):
    qi = pl.program_id(2)
    per = LANES // HEAD_DIM
    assert per == 2
    row = lax.broadcasted_iota(jnp.int32, (blk, blk), 0)
    col = lax.broadcasted_iota(jnp.int32, (blk, blk), 1)
    causal = col < row
    later = jnp.where(row >= col, 1.0, 0.0).astype(BF16)
    later2 = jnp.concatenate([later, later], axis=0)
    first = lax.broadcasted_iota(jnp.int32, (blk, LANES), 1) < HEAD_DIM
    heads = [(p, h) for p in range(n_pairs) for h in range(per)]
    q = []
    for p, h in heads:
        qp = q_ref[0, :, p * LANES:(p + 1) * LANES]
        q.append(jnp.where(first if h == 0 else ~first, qp, jnp.zeros_like(qp)))

    def block(j, carry):
        start = pl.multiple_of(j * blk, blk)
        rows = pl.ds(start, blk)
        n = len(heads)
        z, split, rc, w, pv = [None] * n, [None] * n, [None] * n, [None] * n, [None] * n

        def scores(i):
            p = heads[i][0]
            z[i] = _dot_nt(q[i], k_ref[0, rows, p * LANES:(p + 1) * LANES])

        def keep_terms(i):
            sp = jnp.maximum(z[i], 0.0) + jnp.log(1.0 + jnp.exp2(_neg_abs(z[i]))) * LOG2E
            if carry is None:
                sp = jnp.where(causal, sp, 0.0)
            hi = sp.astype(BF16)
            split[i] = jnp.concatenate([hi, (sp - hi.astype(F32)).astype(BF16)], axis=1)

        def cumulate(i):
            rc[i] = _dot(split[i], later2)

        def weights(i):
            if carry is None:
                w[i] = jnp.where(causal, jnp.exp2(z[i] - rc[i]), 0.0).astype(BF16)
            else:
                w[i] = jnp.exp2(z[i] - rc[i] - carry[i]).astype(BF16)

        def values(i):
            p = heads[i][0]
            pv[i] = _dot(w[i], v_ref[0, rows, p * LANES:(p + 1) * LANES])

        _emit_skewed(n, (scores, keep_terms, cumulate, weights, values))
        return pv, [r[:, 0:1] for r in rc]

    pv, carry = block(qi, None)
    for i, x in enumerate(pv):
        acc_ref[i] = x

    def step(it, carry):
        pv, tot = block(qi - 1 - it, carry)
        for i, x in enumerate(pv):
            acc_ref[i] += x
        return tuple(c + t for c, t in zip(carry, tot))

    lax.fori_loop(0, qi, step, tuple(carry))
    for p in range(n_pairs):
        o_ref[0, :, p * LANES:(p + 1) * LANES] = jnp.where(first, acc_ref[per * p], acc_ref[per * p + 1])


def _sb_attention(q, kv, blk, n_pairs):
    b, t, _ = q.shape
    width = n_pairs * LANES
    groups = SB_W // width
    return pl.pallas_call(
        functools.partial(_sb_body, blk=blk, n_pairs=n_pairs),
        out_shape=jax.ShapeDtypeStruct((b, t, SB_W), F32),
        grid=(b, groups, t // blk),
        in_specs=[
            pl.BlockSpec((1, blk, width), lambda bi, g, qi: (bi, qi, g)),
            pl.BlockSpec((1, t, width), lambda bi, g, qi: (bi, 0, g)),
            pl.BlockSpec((1, t, width), lambda bi, g, qi: (bi, 0, groups + g)),
        ],
        out_specs=pl.BlockSpec((1, blk, width), lambda bi, g, qi: (bi, qi, g)),
        scratch_shapes=[pltpu.VMEM((2 * n_pairs, blk, LANES), F32)],
        compiler_params=_params(("parallel", "parallel", "arbitrary")),
        name="sb_attention",
    )(q, kv, kv)


def _half_rms_scale(x, first):
    sq = x * x
    ms0 = jnp.sum(jnp.where(first, sq, 0.0), axis=-1, keepdims=True) * (1.0 / HEAD_DIM)
    ms1 = jnp.sum(jnp.where(first, 0.0, sq), axis=-1, keepdims=True) * (1.0 / HEAD_DIM)
    return jnp.where(first, lax.rsqrt(ms0 + EPS), lax.rsqrt(ms1 + EPS))


def _swa_body(sink_ref, q_ref, kvc_ref, kvp_ref, qg_ref, kg_ref, o_ref, *, n_blocks, slopes):
    tile = pl.program_id(1)
    w = WINDOW
    group = SWA_HEADS // SWA_KV_HEADS
    per = LANES // HEAD_DIM
    assert per == 2 and SWA_KVW == LANES
    first = lax.broadcasted_iota(jnp.int32, (1, LANES), 1) < HEAD_DIM

    def both_halves(x):
        swapped = pltpu.roll(x, HEAD_DIM, axis=1)
        return jnp.where(first, x, swapped), jnp.where(first, swapped, x)

    kv = jnp.concatenate([kvp_ref[0], kvc_ref[0]], axis=0)
    k = kv[:, :SWA_KVW]
    k = k * _half_rms_scale(k, first) * kg_ref[...]
    k_dup = [x.astype(BF16) for x in both_halves(k)]
    v_dup = [x.astype(BF16) for x in both_halves(kv[:, SWA_KVW:])]

    q_heads = []
    for g2 in range(SWA_QW // LANES):
        q = q_ref[0, :, g2 * LANES:(g2 + 1) * LANES]
        q = q * _half_rms_scale(q, first) * (qg_ref[...] * (HEAD_DIM ** -0.5 * LOG2E))
        q_heads.append(jnp.where(first, q, 0.0).astype(BF16))
        q_heads.append(jnp.where(first, 0.0, q).astype(BF16))

    qpos = lax.broadcasted_iota(jnp.int32, (w, 2 * w), 0)
    kpos = lax.broadcasted_iota(jnp.int32, (w, 2 * w), 1)
    dist = qpos + w - kpos
    band = (dist >= 0) & (dist < w)
    distf = dist.astype(F32)
    has_prev = (tile > 0) | (kpos >= w)
    bias, sink = [], []
    for kvh in range(SWA_KV_HEADS):
        heads = range(kvh * group, (kvh + 1) * group)
        bias.append(jnp.concatenate(
            [jnp.where(band, (-slopes[h] * LOG2E) * distf, -jnp.inf) for h in heads], axis=0))
        sink.append(jnp.concatenate(
            [jnp.full((w, 1), sink_ref[h] * LOG2E, F32) for h in heads], axis=0))

    units = [(j, kvh) for j in range(n_blocks) for kvh in range(SWA_KV_HEADS)]
    n = len(units)
    s, e, denom, res = [None] * n, [None] * n, [None] * n, [None] * n

    def scores(i):
        j, kvh = units[i]
        q4 = jnp.concatenate([q_heads[kvh * group + g][j * w:(j + 1) * w] for g in range(group)], axis=0)
        s[i] = _dot_nt(q4, k_dup[kvh][j * w:(j + 2) * w]) + bias[kvh]
        if j == 0:
            s[i] = jnp.where(jnp.concatenate([has_prev] * group, axis=0), s[i], -jnp.inf)

    def softmax(i):
        kvh = units[i][1]
        m = jnp.maximum(jnp.max(s[i], axis=-1, keepdims=True), sink[kvh])
        p = jnp.exp2(s[i] - m)
        denom[i] = jnp.sum(p, axis=-1, keepdims=True) + jnp.exp2(sink[kvh] - m)
        e[i] = p.astype(BF16)

    def values(i):
        j, kvh = units[i]
        res[i] = _dot(e[i], v_dup[kvh][j * w:(j + 2) * w]) / denom[i]

    def store(i):
        j, kvh = units[i]
        for g2 in range(group // per):
            piece = [res[i][(per * g2 + h) * w:(per * g2 + h + 1) * w] for h in range(per)]
            lanes = (kvh * (group // per) + g2) * LANES
            o_ref[0, j * w:(j + 1) * w, lanes:lanes + LANES] = jnp.where(first, piece[0], piece[1])

    _emit_skewed(n, (scores, softmax, values, store))


def _swa_attention(proj, q_gain, k_gain, sinks, n_blocks):
    b, t, _ = proj.shape
    w = WINDOW
    tq = n_blocks * w
    kv_blk = SWA_QW // (2 * SWA_KVW)
    slopes = tuple(float(2.0 ** (-8.0 * (i + 1) / SWA_HEADS)) for i in range(SWA_HEADS))
    gain2 = lambda g: jnp.concatenate([g, g]).reshape(1, LANES)
    return pl.pallas_call(
        functools.partial(_swa_body, n_blocks=n_blocks, slopes=slopes),
        out_shape=jax.ShapeDtypeStruct((b, t, SWA_QW), F32),
        grid=(b, t // tq),
        in_specs=[
            pl.BlockSpec(memory_space=pltpu.SMEM),
            pl.BlockSpec((1, tq, SWA_QW), lambda bi, i: (bi, i, 0)),
            pl.BlockSpec((1, tq, 2 * SWA_KVW), lambda bi, i: (bi, i, kv_blk)),
            pl.BlockSpec((1, w, 2 * SWA_KVW), lambda bi, i: (bi, jnp.maximum(i * n_blocks - 1, 0), kv_blk)),
            pl.BlockSpec((1, LANES), lambda bi, i: (0, 0)),
            pl.BlockSpec((1, LANES), lambda bi, i: (0, 0)),
        ],
        out_specs=pl.BlockSpec((1, tq, SWA_QW), lambda bi, i: (bi, i, 0)),
        compiler_params=_params(("parallel", "parallel")),
        name="swa_attention",
    )(sinks, proj, proj, proj, gain2(q_gain), gain2(k_gain))


def _gdn_in_body(x_ref, halo_ref, g_ref, w_ref, wba_ref, cw_ref, alog_ref, dtb_ref,
                 qkv_ref, z_ref, gb_ref, xn_ref, pre_ref, *, tm, cols, sub):
    ti = pl.program_id(1)
    step = pl.program_id(2)
    conv_steps = GDN_CONV_W // cols
    qk_steps = GDN_KW // cols
    nh = GDN_V_HEADS

    @pl.when(step == 0)
    def _():
        xh = halo_ref[0]
        x = x_ref[0]
        xn_ref[0:CONV_HALO, :] = (xh * _rms_scale(xh) * g_ref[...]).astype(BF16)
        xn_ref[CONV_HALO:, :] = (x * _rms_scale(x) * g_ref[...]).astype(BF16)
        ba = _dot(xn_ref[CONV_HALO:, :], wba_ref[...])
        beta = _sigmoid(ba[:, :nh])
        g = -jnp.exp(alog_ref[...]) * _softplus(ba[:, nh:2 * nh] + dtb_ref[...])
        r = lax.broadcasted_iota(jnp.int32, (GDN_CHUNK, GDN_CHUNK), 0)
        c = lax.broadcasted_iota(jnp.int32, (GDN_CHUNK, GDN_CHUNK), 1)
        lower = jnp.where(r >= c, 1.0, 0.0).astype(F32)
        for chunk in range(tm // GDN_CHUNK):
            rows = slice(chunk * GDN_CHUNK, (chunk + 1) * GDN_CHUNK)
            gb_ref[0, rows, :] = jnp.concatenate([_dot_f32(lower, g[rows]), beta[rows]], axis=-1)

    @pl.when(step < conv_steps)
    def _():
        is_q = step < qk_steps
        is_k = (step >= qk_steps) & (step < 2 * qk_steps)

        def project(i):
            pre = _dot(xn_ref[...], w_ref[:, i * sub:(i + 1) * sub])
            pre_ref[i, 0:CONV_HALO, :] = jnp.where(ti > 0, pre[:CONV_HALO], 0.0)
            pre_ref[i, CONV_HALO:, :] = pre[CONV_HALO:]

        def conv(i):
            ext = pre_ref[i]
            y = None
            for tap in range(GDN_CONV):
                back = GDN_CONV - 1 - tap
                rows = ext if back == 0 else pltpu.roll(ext, back, axis=0)
                term = rows[CONV_HALO:] * cw_ref[tap:tap + 1, i * sub:(i + 1) * sub]
                y = term if y is None else y + term
            y = y / (1.0 + jnp.exp2(y * (-LOG2E)))
            for head in range(sub // GDN_HEAD_DIM):
                lanes = slice(head * GDN_HEAD_DIM, (head + 1) * GDN_HEAD_DIM)
                yh = y[:, lanes]
                inv = lax.rsqrt(jnp.sum(yh * yh, axis=-1, keepdims=True) + EPS)
                mult = jnp.where(is_q, inv * (GDN_HEAD_DIM ** -0.5), jnp.where(is_k, inv, 1.0))
                qkv_ref[0, :, i * sub + head * GDN_HEAD_DIM:i * sub + (head + 1) * GDN_HEAD_DIM] = yh * mult

        _emit_skewed(cols // sub, (project, conv))

    @pl.when(step >= conv_steps)
    def _():
        z_ref[0] = _dot(xn_ref[CONV_HALO:, :], w_ref[...])


def _gdn_in(h3, gain, w, conv_w, a_log, dt_bias, tm):
    b, t, d = h3.shape
    nh = GDN_V_HEADS
    cols = GDN_IN_COLS
    conv_steps = GDN_CONV_W // cols
    steps = (GDN_CONV_W + GDN_VW) // cols
    w_ba = w[:, GDN_CONV_W + GDN_VW:]
    return pl.pallas_call(
        functools.partial(_gdn_in_body, tm=tm, cols=cols, sub=GDN_IN_SUB),
        out_shape=[
            jax.ShapeDtypeStruct((b, t, GDN_CONV_W), F32),
            jax.ShapeDtypeStruct((b, t, GDN_VW), F32),
            jax.ShapeDtypeStruct((b, t, 2 * nh), F32),
        ],
        grid=(b, t // tm, steps),
        in_specs=[
            pl.BlockSpec((1, tm, d), lambda bi, ti, s: (bi, ti, 0)),
            pl.BlockSpec((1, CONV_HALO, d),
                         lambda bi, ti, s: (bi, jnp.maximum(ti * (tm // CONV_HALO) - 1, 0), 0)),
            pl.BlockSpec((1, d), lambda bi, ti, s: (0, 0)),
            pl.BlockSpec((d, cols), lambda bi, ti, s: (0, s)),
            pl.BlockSpec(w_ba.shape, lambda bi, ti, s: (0, 0)),
            pl.BlockSpec((GDN_CONV, cols), lambda bi, ti, s: (0, jnp.minimum(s, conv_steps - 1))),
            pl.BlockSpec((1, nh), lambda bi, ti, s: (0, 0)),
            pl.BlockSpec((1, nh), lambda bi, ti, s: (0, 0)),
        ],
        out_specs=[
            pl.BlockSpec((1, tm, cols), lambda bi, ti, s: (bi, ti, jnp.minimum(s, conv_steps - 1))),
            pl.BlockSpec((1, tm, cols), lambda bi, ti, s: (bi, ti, jnp.maximum(s - conv_steps, 0))),
            pl.BlockSpec((1, tm, 2 * nh), lambda bi, ti, s: (bi, ti, 0)),
        ],
        scratch_shapes=[
            pltpu.VMEM((CONV_HALO + tm, d), BF16),
            pltpu.VMEM((cols // GDN_IN_SUB, CONV_HALO + tm, GDN_IN_SUB), F32),
        ],
        compiler_params=_params(("parallel", "parallel", "arbitrary")),
        name="gdn_in",
    )(h3, h3, gain.reshape(1, d), w, w_ba, conv_w, a_log.reshape(1, nh), dt_bias.reshape(1, nh))


def _split(x):
    hi = x.astype(BF16)
    return hi, (x - hi.astype(F32)).astype(BF16)


def _block_diag(y, mask):
    return jnp.where(mask, jnp.concatenate([y, y], axis=0), jnp.zeros((), y.dtype))


def _pair_lhs(x):
    hi, lo = _split(x)
    return jnp.concatenate([hi, hi, lo], axis=1)


def _pair_rhs(y, mask):
    hi, lo = _split(y)
    hi, lo = _block_diag(hi, mask), _block_diag(lo, mask)
    return jnp.concatenate([hi, lo, hi], axis=0)


def _pair_mm(lhs, rhs):
    return _dot(lhs, rhs)


def _unit_lower_inverse_pairs(a_list, eye, same_block, bd_mask):
    mm = lambda xs, ys: [_pair_mm(_pair_lhs(x), _pair_rhs(y, bd_mask)) for x, y in zip(xs, ys)]
    add = lambda xs, ys: [x + y for x, y in zip(xs, ys)]
    c = a_list[0].shape[0]
    ad = [jnp.where(same_block, a, 0.0) for a in a_list]
    lo = [a - d for a, d in zip(a_list, ad)]
    dinv = [eye - d for d in ad]
    power = ad
    span = 2
    while span < INV_BLOCK:
        power = mm(power, power)
        dinv = add(dinv, mm(dinv, power))
        span *= 2
    n = mm(dinv, lo)
    m = [eye - x for x in n]
    power = n
    span = 2
    while span < c // INV_BLOCK:
        power = mm(power, power)
        m = add(m, mm(m, power))
        span *= 2
    return mm(m, dinv)


def _gdn_local_body(q_ref, k_ref, v_ref, gb_ref, gr_ref, u_ref, wq_ref, kt_ref, at_ref, *, n_chunks):
    pair = pl.program_id(1)
    cs = GDN_CHUNK
    nh = GDN_V_HEADS
    d = GDN_HEAD_DIM
    rep = GDN_V_HEADS // GDN_K_HEADS
    assert rep == 2 and rep * cs == LANES

    ii = lax.broadcasted_iota(jnp.int32, (cs, rep * cs), 0)
    lane = lax.broadcasted_iota(jnp.int32, (cs, rep * cs), 1)
    jj = lane % cs
    first = lane < cs
    eye = jnp.where(ii == jj, 1.0, 0.0).astype(F32)
    same_block = (ii // INV_BLOCK) == (jj // INV_BLOCK)
    r2 = lax.broadcasted_iota(jnp.int32, (rep * cs, rep * cs), 0)
    c2 = lax.broadcasted_iota(jnp.int32, (rep * cs, rep * cs), 1)
    bd_mask = (r2 // cs) == (c2 // cs)
    lane_gb = lax.broadcasted_iota(jnp.int32, (cs, 2 * nh), 1)
    zeros_rhs = jnp.zeros((cs, 2 * d), BF16)
    chunks = range(n_chunks)
    rows = [slice(c * cs, (c + 1) * cs) for c in chunks]

    def column(gb, idx):
        return jnp.sum(jnp.where(lane_gb == idx, gb, 0.0), axis=-1, keepdims=True)

    q = [q_ref[0, r, :] for r in rows]
    k = [k_ref[0, r, :] for r in rows]
    gb = [gb_ref[0, r, :] for r in rows]
    gc = [[column(g, rep * pair + h) for h in range(rep)] for g in gb]
    beta = [[column(g, nh + rep * pair + h) for h in range(rep)] for g in gb]
    gc_row = [gr_ref[0, 0, c] for c in chunks]
    decay = [jnp.exp(jnp.where(ii >= jj, jnp.where(first, gc[c][0], gc[c][1]) - gc_row[c], -jnp.inf))
             for c in chunks]
    k16 = [x.astype(BF16) for x in k]
    kk16 = [jnp.concatenate([x, x], axis=0) for x in k16]
    a = [jnp.where(ii > jj, _dot_nt(k16[c], kk16[c]) * jnp.where(first, beta[c][0], beta[c][1]) * decay[c], 0.0)
         for c in chunks]
    for c in chunks:
        at_ref[0, 0, rows[c], :] = jnp.where(
            ii >= jj, _dot_nt(q[c].astype(BF16), kk16[c]) * decay[c], 0.0).astype(BF16)
    tinv = _unit_lower_inverse_pairs(a, eye, same_block, bd_mask)

    eg = [[jnp.exp(gc[c][h]) for h in range(rep)] for c in chunks]
    sol = []
    for c in chunks:
        rhs_hi, rhs_lo = [], []
        for h in range(rep):
            v = v_ref[0, rows[c], h * d:(h + 1) * d]
            hi, lo = _split(jnp.concatenate([v * beta[c][h], k[c] * (beta[c][h] * eg[c][h])], axis=-1))
            pad = [zeros_rhs] * rep
            rhs_hi.append(jnp.concatenate(pad[:h] + [hi] + pad[h + 1:], axis=1))
            rhs_lo.append(jnp.concatenate(pad[:h] + [lo] + pad[h + 1:], axis=1))
        rhs_hi = jnp.concatenate(rhs_hi, axis=0)
        rhs_lo = jnp.concatenate(rhs_lo, axis=0)
        sol.append(_dot(_pair_lhs(tinv[c]), jnp.concatenate([rhs_hi, rhs_lo, rhs_hi], axis=0)))

    for c in chunks:
        for h in range(rep):
            u_ref[0, rows[c], h * d:(h + 1) * d] = sol[c][:, 2 * d * h:2 * d * h + d]
            wq_ref[0, h, c, :cs, :] = sol[c][:, 2 * d * h + d:2 * d * (h + 1)].astype(BF16)
            wq_ref[0, h, c, cs:, :] = (q[c] * eg[c][h]).astype(BF16)
            g_last = gc_row[c][:, h * cs + cs - 1:(h + 1) * cs]
            kt_ref[0, h, c] = (k[c] * jnp.exp(g_last - gc[c][h])).T.astype(BF16)


def _gdn_local(qkv, gb, gc_rows, n_chunks):
    b, t, _ = qkv.shape
    nh = GDN_V_HEADS
    rep = GDN_V_HEADS // GDN_K_HEADS
    cs = GDN_CHUNK
    tt = n_chunks * cs
    d = GDN_HEAD_DIM
    return pl.pallas_call(
        functools.partial(_gdn_local_body, n_chunks=n_chunks),
        out_shape=[
            jax.ShapeDtypeStruct((b, t, GDN_VW), F32),
            jax.ShapeDtypeStruct((b, nh, t // cs, 2 * cs, d), BF16),
            jax.ShapeDtypeStruct((b, nh, t // cs, d, cs), BF16),
            jax.ShapeDtypeStruct((b, GDN_K_HEADS, t, rep * cs), BF16),
        ],
        grid=(b, GDN_K_HEADS, t // tt),
        in_specs=[
            pl.BlockSpec((1, tt, d), lambda bi, p, ti: (bi, ti, p)),
            pl.BlockSpec((1, tt, d), lambda bi, p, ti: (bi, ti, GDN_K_HEADS + p)),
            pl.BlockSpec((1, tt, rep * d), lambda bi, p, ti: (bi, ti, GDN_K_HEADS + p)),
            pl.BlockSpec((1, tt, 2 * nh), lambda bi, p, ti: (bi, ti, 0)),
            pl.BlockSpec((1, 1, n_chunks, 1, rep * cs), lambda bi, p, ti: (bi, p, ti, 0, 0)),
        ],
        out_specs=[
            pl.BlockSpec((1, tt, rep * d), lambda bi, p, ti: (bi, ti, p)),
            pl.BlockSpec((1, rep, n_chunks, 2 * cs, d), lambda bi, p, ti: (bi, p, ti, 0, 0)),
            pl.BlockSpec((1, rep, n_chunks, d, cs), lambda bi, p, ti: (bi, p, ti, 0, 0)),
            pl.BlockSpec((1, 1, tt, rep * cs), lambda bi, p, ti: (bi, p, ti, 0)),
        ],
        compiler_params=_params(("parallel", "parallel", "parallel")),
        name="gdn_local",
    )(qkv, qkv, qkv, gb, gc_rows)


def _gdn_scan_body(u_ref, wq_ref, kt_ref, at_ref, gr_ref, o_ref, s_ref, *, n_chunks, n_pairs):
    cs = GDN_CHUNK
    d = GDN_HEAD_DIM
    rep = GDN_V_HEADS // GDN_K_HEADS
    heads = range(rep * n_pairs)

    @pl.when(pl.program_id(2) == 0)
    def _():
        s_ref[...] = jnp.zeros_like(s_ref)

    zeros_v = jnp.zeros((cs, d), BF16)
    states = [s_ref[i] for i in heads]
    for c in range(n_chunks):
        rows = slice(c * cs, (c + 1) * cs)
        s16 = [s.astype(BF16) for s in states]
        both = [_dot(wq_ref[0, i, c], s16[i]) for i in heads]
        v_new = [(u_ref[0, rows, i * d:(i + 1) * d] - both[i][:cs]).astype(BF16) for i in heads]
        for p in range(n_pairs):
            gc_row = gr_ref[0, p, c]
            diag = []
            for h in range(rep):
                pad = [zeros_v] * rep
                diag.append(jnp.concatenate(pad[:h] + [v_new[rep * p + h]] + pad[h + 1:], axis=1))
            o_ref[0, rows, rep * p * d:rep * (p + 1) * d] = (
                jnp.concatenate([both[rep * p + h][cs:] for h in range(rep)], axis=1)
                + _dot(at_ref[0, p, rows, :], jnp.concatenate(diag, axis=0)))
            for h in range(rep):
                i = rep * p + h
                g_last = gc_row[:, h * cs + cs - 1:(h + 1) * cs]
                states[i] = states[i] * jnp.exp(g_last) + _dot(kt_ref[0, i, c], v_new[i])
    for i in heads:
        s_ref[i] = states[i]


def _gdn_scan(u, wq, kt, attn, gc_rows, n_chunks, n_pairs):
    b, t, _ = u.shape
    rep = GDN_V_HEADS // GDN_K_HEADS
    cs = GDN_CHUNK
    tt = n_chunks * cs
    d = GDN_HEAD_DIM
    nhs = rep * n_pairs
    return pl.pallas_call(
        functools.partial(_gdn_scan_body, n_chunks=n_chunks, n_pairs=n_pairs),
        out_shape=jax.ShapeDtypeStruct((b, t, GDN_VW), F32),
        grid=(b, GDN_K_HEADS // n_pairs, t // tt),
        in_specs=[
            pl.BlockSpec((1, tt, nhs * d), lambda bi, g, ti: (bi, ti, g)),
            pl.BlockSpec((1, nhs, n_chunks, 2 * cs, d), lambda bi, g, ti: (bi, g, ti, 0, 0)),
            pl.BlockSpec((1, nhs, n_chunks, d, cs), lambda bi, g, ti: (bi, g, ti, 0, 0)),
            pl.BlockSpec((1, n_pairs, tt, rep * cs), lambda bi, g, ti: (bi, g, ti, 0)),
            pl.BlockSpec((1, n_pairs, n_chunks, 1, rep * cs), lambda bi, g, ti: (bi, g, ti, 0, 0)),
        ],
        out_specs=pl.BlockSpec((1, tt, nhs * d), lambda bi, g, ti: (bi, ti, g)),
        scratch_shapes=[pltpu.VMEM((nhs, d, d), F32)],
        compiler_params=_params(("parallel", "parallel", "arbitrary")),
        name="gdn_scan",
    )(u, wq, kt, attn, gc_rows)


def _gdn_out_body(h_ref, o_ref_in, z_ref, g_ref, w_ref, out_ref, y_ref):
    for head in range(GDN_V_HEADS):
        lanes = slice(head * GDN_HEAD_DIM, (head + 1) * GDN_HEAD_DIM)
        o = o_ref_in[:, lanes]
        z = z_ref[:, lanes]
        y_ref[:, lanes] = (o * _rms_scale(o) * g_ref[...] * (z * _sigmoid(z))).astype(BF16)
    out_ref[...] = h_ref[...] + _dot(y_ref[...], w_ref[...])


def _gdn_out(h, o, z, gain, w, tm):
    n, d = h.shape
    row = lambda i: (i, 0)
    const = lambda i: (0, 0)
    return pl.pallas_call(
        _gdn_out_body,
        out_shape=jax.ShapeDtypeStruct((n, d), F32),
        grid=(n // tm,),
        in_specs=[
            pl.BlockSpec((tm, d), row),
            pl.BlockSpec((tm, GDN_VW), row),
            pl.BlockSpec((tm, GDN_VW), row),
            pl.BlockSpec((1, GDN_HEAD_DIM), const),
            pl.BlockSpec(w.shape, const, pipeline_mode=pl.Buffered(1)),
        ],
        out_specs=pl.BlockSpec((tm, d), row),
        scratch_shapes=[pltpu.VMEM((tm, GDN_VW), BF16)],
        compiler_params=_params(("parallel",)),
        name="gdn_out",
    )(h, o, z, gain.reshape(1, GDN_HEAD_DIM), w)


def _attention_mixer(h, b, t, gain, w_in, q_gain, k_gain, sinks, w_out, tm):
    d = h.shape[1]
    swa_w = SWA_QW + 2 * SWA_KVW
    outs = ((SB_W, BF16, HEAD_DIM ** -0.5 * LOG2E), (2 * SB_W, BF16, None), (swa_w, F32, None))
    sb_q, sb_kv, swa = _norm_proj(h, gain, w_in.astype(BF16), outs, tm)
    a_out = _sb_attention(sb_q.reshape(b, t, SB_W), sb_kv.reshape(b, t, 2 * SB_W),
                          min(SB_BLOCK, t), SB_PAIRS).reshape(b * t, SB_W)
    b_out = _swa_attention(swa.reshape(b, t, swa_w), q_gain, k_gain, sinks,
                           min(SWA_TILE_BLOCKS, t // WINDOW)).reshape(b * t, SWA_QW)
    return _att_out(h, a_out, b_out, w_out.astype(BF16), tm)


def _gdn_mixer(h, b, t, gain, w_in, conv_w, a_log, dt_bias, out_gain, w_out, tm):
    nh = GDN_V_HEADS
    ba_w = w_in.shape[1] - GDN_CONV_W - GDN_VW
    assert ba_w == 2 * nh
    qkv, z, gb = _gdn_in(h.reshape(b, t, -1), gain, w_in.astype(BF16), conv_w, a_log, dt_bias, min(tm, t))
    z = z.reshape(b * t, GDN_VW)
    rep = GDN_V_HEADS // GDN_K_HEADS
    gc_rows = gb[:, :, :nh].reshape(b, t // GDN_CHUNK, GDN_CHUNK, GDN_K_HEADS, rep)
    gc_rows = jnp.transpose(gc_rows, (0, 3, 1, 4, 2)).reshape(
        b, GDN_K_HEADS, t // GDN_CHUNK, 1, rep * GDN_CHUNK)
    u, wq, kt, attn = _gdn_local(qkv, gb, gc_rows, min(GDN_LOCAL_CHUNKS, t // GDN_CHUNK))
    o = _gdn_scan(u, wq, kt, attn, gc_rows, min(GDN_SCAN_CHUNKS, t // GDN_CHUNK), GDN_SCAN_PAIRS)
    return _gdn_out(h, o.reshape(b * t, GDN_VW), z, out_gain, w_out.astype(BF16), tm)


def kernel(x, p, ffn_norm, ffn_w_gate, ffn_w_up, ffn_w_down, mix_norm, att_w_in, att_q_norm, att_k_norm, att_sinks, att_w_out, gdn_w_in, gdn_conv_w, gdn_a_log, gdn_dt_bias, gdn_out_norm, gdn_w_out, ple_norm, ple_w_gate, ple_w_proj):
    b, t, d = x.shape
    n = b * t
    tm = min(TOKEN_TILE, n)
    depth = p.shape[0]
    h = x.reshape(n, d)
    for i in range(depth):
        j = i // 2
        h = _ffn(h, ffn_norm[i, 0], ffn_w_gate[i, 0].astype(BF16), ffn_w_up[i, 0].astype(BF16),
                 ffn_w_down[i, 0].astype(BF16), tm)
        if i % 2 == 0:
            h = _attention_mixer(h, b, t, mix_norm[i], att_w_in[j], att_q_norm[j], att_k_norm[j],
                                 att_sinks[j], att_w_out[j], tm)
        else:
            h = _gdn_mixer(h, b, t, mix_norm[i], gdn_w_in[j], gdn_conv_w[j], gdn_a_log[j],
                           gdn_dt_bias[j], gdn_out_norm[j], gdn_w_out[j], tm)
        h = _ffn(h, ffn_norm[i, 1], ffn_w_gate[i, 1].astype(BF16), ffn_w_up[i, 1].astype(BF16),
                 ffn_w_down[i, 1].astype(BF16), tm)
        h = _ple(h, p.reshape(depth, n, -1), i, ple_norm[i], ple_w_gate[i].astype(BF16),
                 ple_w_proj[i].astype(BF16), tm)
    return h.reshape(b, t, d)
```

```python
import functools

import jax
import jax.numpy as jnp
from jax import lax
from jax.experimental import pallas as pl
from jax.experimental.pallas import tpu as pltpu

F32 = jnp.float32
BF16 = jnp.bfloat16

EPS = 1e-6
LOG2E = 1.4426950408889634
HEAD_DIM = 64
SB_HEADS = 8
SWA_HEADS = 8
SWA_KV_HEADS = 2
WINDOW = 128
GDN_K_HEADS = 8
GDN_V_HEADS = 16
GDN_HEAD_DIM = 128
GDN_CONV = 4
GDN_CHUNK = 64
SB_W = SB_HEADS * HEAD_DIM
SWA_QW = SWA_HEADS * HEAD_DIM
SWA_KVW = SWA_KV_HEADS * HEAD_DIM
GDN_KW = GDN_K_HEADS * GDN_HEAD_DIM
GDN_VW = GDN_V_HEADS * GDN_HEAD_DIM
GDN_CONV_W = 2 * GDN_KW + GDN_VW

LANES = 128
VMEM_LIMIT = 56 * 1024 * 1024

TOKEN_TILE = 512
FF_CHUNK = 256
SB_BLOCK = 256
SB_Q_BLOCKS = 2
SB_CUMSUM_TERMS = 1
SB_PAIRS = 4
SWA_TILE_BLOCKS = 4
GDN_IN_TILE = 1024
GDN_IN_COLS = 1024
GDN_IN_SUB = 256
CONV_HALO = 16
GDN_LOCAL_CHUNKS = 8
GDN_SCAN_CHUNKS = 4
GDN_SCAN_PAIRS = 4
INV_BLOCK = 16


def _params(sem):
    return pltpu.CompilerParams(dimension_semantics=sem, vmem_limit_bytes=VMEM_LIMIT)


def _dot(a, b):
    return jnp.dot(a, b, preferred_element_type=F32)


def _dot_nt(a, b):
    return lax.dot_general(a, b, (((1,), (1,)), ((), ())), preferred_element_type=F32)


def _dot_f32(a, b):
    return jnp.dot(a, b, preferred_element_type=F32, precision=lax.Precision.HIGHEST)


def _rms_scale(x):
    return lax.rsqrt(jnp.mean(x * x, axis=-1, keepdims=True) + EPS)


def _sigmoid(x):
    return 1.0 / (1.0 + jnp.exp(-x))


def _emit_skewed(n, stages):
    for tick in range(n + len(stages) - 1):
        for s, stage in enumerate(stages):
            if 0 <= tick - s < n:
                stage(tick - s)


def _neg_abs(x):
    bits = lax.bitcast_convert_type(x, jnp.uint32) | jnp.uint32(0x80000000)
    return lax.bitcast_convert_type(bits, F32)


def _softplus(x):
    return jnp.maximum(x, 0.0) + jnp.log1p(jnp.exp(-jnp.abs(x)))


def _ffn_body(x_ref, g_ref, wg_ref, wu_ref, wd_ref, o_ref, acc_ref, *, n_chunks, fc):
    x = x_ref[...]
    xn = (x * _rms_scale(x) * g_ref[...]).astype(BF16)
    for c in range(n_chunks):
        cols = slice(c * fc, (c + 1) * fc)
        gate = _dot(xn, wg_ref[:, cols])
        up = _dot(xn, wu_ref[:, cols])
        act = (gate * _sigmoid(gate) * up).astype(BF16)
        part = _dot(act, wd_ref[cols, :])
        if c == 0:
            acc_ref[...] = part
        else:
            acc_ref[...] += part
    o_ref[...] = x + 0.5 * acc_ref[...]


def _ffn(h, gain, wg, wu, wd, layer, half, tm):
    n, d = h.shape
    f = wg.shape[-1]
    fc = FF_CHUNK if f % FF_CHUNK == 0 else f
    const = lambda i: (0, 0)
    pick = lambda i: (layer, half, 0, 0)
    return pl.pallas_call(
        functools.partial(_ffn_body, n_chunks=f // fc, fc=fc),
        out_shape=jax.ShapeDtypeStruct((n, d), F32),
        grid=(n // tm,),
        in_specs=[
            pl.BlockSpec((tm, d), lambda i: (i, 0)),
            pl.BlockSpec((1, d), const),
            pl.BlockSpec((None, None, d, f), pick, pipeline_mode=pl.Buffered(1)),
            pl.BlockSpec((None, None, d, f), pick, pipeline_mode=pl.Buffered(1)),
            pl.BlockSpec((None, None, f, d), pick, pipeline_mode=pl.Buffered(1)),
        ],
        out_specs=pl.BlockSpec((tm, d), lambda i: (i, 0)),
        scratch_shapes=[pltpu.VMEM((tm, d), F32)],
        compiler_params=_params(("parallel",)),
        name="ffn",
    )(h, gain.reshape(1, d), wg, wu, wd)


def _norm_proj_body(x_ref, g_ref, w_ref, *o_refs, outs):
    x = x_ref[...]
    xn = (x * _rms_scale(x) * g_ref[...]).astype(BF16)
    start = 0
    for o_ref, (width, _, scale) in zip(o_refs, outs):
        y = _dot(xn, w_ref[:, start:start + width])
        o_ref[...] = (y if scale is None else y * scale).astype(o_ref.dtype)
        start += width


def _norm_proj(h, gain, w, outs, tm):
    n, d = h.shape
    widths = [o[0] for o in outs]
    assert sum(widths) == w.shape[1]
    return pl.pallas_call(
        functools.partial(_norm_proj_body, outs=outs),
        out_shape=[jax.ShapeDtypeStruct((n, width), dtype) for width, dtype, _ in outs],
        grid=(n // tm,),
        in_specs=[
            pl.BlockSpec((tm, d), lambda i: (i, 0)),
            pl.BlockSpec((1, d), lambda i: (0, 0)),
            pl.BlockSpec(w.shape, lambda i: (0, 0), pipeline_mode=pl.Buffered(1)),
        ],
        out_specs=[pl.BlockSpec((tm, width), lambda i: (i, 0)) for width in widths],
        compiler_params=_params(("parallel",)),
        name="norm_proj",
    )(h, gain.reshape(1, d), w)


def _mixer_out_body(h_ref, *refs):
    x_refs, w_ref, o_ref = refs[:-2], refs[-2], refs[-1]
    acc = h_ref[...]
    start = 0
    for x_ref in x_refs:
        width = x_ref.shape[1]
        acc = acc + _dot(x_ref[...].astype(BF16), w_ref[start:start + width, :])
        start += width
    o_ref[...] = acc


def _mixer_out(h, xs, w, tm):
    n, d = h.shape
    row = lambda i: (i, 0)
    assert sum(x.shape[1] for x in xs) == w.shape[0]
    return pl.pallas_call(
        _mixer_out_body,
        out_shape=jax.ShapeDtypeStruct((n, d), F32),
        grid=(n // tm,),
        in_specs=[pl.BlockSpec((tm, d), row)]
        + [pl.BlockSpec((tm, x.shape[1]), row) for x in xs]
        + [pl.BlockSpec(w.shape, lambda i: (0, 0), pipeline_mode=pl.Buffered(1))],
        out_specs=pl.BlockSpec((tm, d), row),
        compiler_params=_params(("parallel",)),
        name="mixer_out",
    )(h, *xs, w)


def _ple_body(x_ref, p_ref, g_ref, wg_ref, wp_ref, o_ref):
    x = x_ref[...]
    xn = (x * _rms_scale(x) * g_ref[...]).astype(BF16)
    gate = _sigmoid(_dot(xn, wg_ref[...]))
    o_ref[...] = x + gate * _dot(p_ref[0].astype(BF16), wp_ref[...])


def _ple(h, p, layer, gain, wg, wp, tm):
    n, d = h.shape
    row = lambda i: (i, 0)
    const = lambda i: (0, 0)
    return pl.pallas_call(
        _ple_body,
        out_shape=jax.ShapeDtypeStruct((n, d), F32),
        grid=(n // tm,),
        in_specs=[
            pl.BlockSpec((tm, d), row),
            pl.BlockSpec((1, tm, p.shape[2]), lambda i: (layer, i, 0)),
            pl.BlockSpec((1, d), const),
            pl.BlockSpec(wg.shape, const, pipeline_mode=pl.Buffered(1)),
            pl.BlockSpec(wp.shape, const, pipeline_mode=pl.Buffered(1)),
        ],
        out_specs=pl.BlockSpec((tm, d), row),
        compiler_params=_params(("parallel",)),
        name="ple",
    )(h, p, gain.reshape(1, d), wg, wp)


def _sb_body(q_ref, k_ref, v_ref, o_ref, acc_ref, *, blk, qsub, n_pairs):
    qi = pl.program_id(2)
    per = LANES // HEAD_DIM
    assert per == 2
    tq = qsub * blk
    row = lax.broadcasted_iota(jnp.int32, (tq, blk), 0)
    col = lax.broadcasted_iota(jnp.int32, (tq, blk), 1)
    later = jnp.where(lax.broadcasted_iota(jnp.int32, (blk, blk), 0)
                      >= lax.broadcasted_iota(jnp.int32, (blk, blk), 1), 1.0, 0.0).astype(BF16)
    later2 = jnp.concatenate([later, later], axis=0)
    first = lax.broadcasted_iota(jnp.int32, (tq, LANES), 1) < HEAD_DIM
    heads = [(p, h) for p in range(n_pairs) for h in range(per)]
    q = []
    for p, h in heads:
        qp = q_ref[0, :, p * LANES:(p + 1) * LANES]
        q.append(jnp.where(first if h == 0 else ~first, qp, jnp.zeros_like(qp)))

    def block(j, carry, diag):
        start = pl.multiple_of(j * blk, blk)
        rows = pl.ds(start, blk)
        causal = None if diag is None else col + diag * blk < row
        n = len(heads)
        z, split, rc, w, pv = [None] * n, [None] * n, [None] * n, [None] * n, [None] * n

        def scores(i):
            p = heads[i][0]
            z[i] = _dot_nt(q[i], k_ref[0, rows, p * LANES:(p + 1) * LANES])

        def keep_terms(i):
            sp = jnp.maximum(z[i], 0.0) + jnp.log(1.0 + jnp.exp2(_neg_abs(z[i]))) * LOG2E
            if causal is not None:
                sp = jnp.where(causal, sp, 0.0)
            hi = sp.astype(BF16)
            if SB_CUMSUM_TERMS == 1:
                split[i] = hi
            else:
                split[i] = jnp.concatenate([hi, (sp - hi.astype(F32)).astype(BF16)], axis=1)

        def cumulate(i):
            rc[i] = _dot(split[i], later if SB_CUMSUM_TERMS == 1 else later2)

        def weights(i):
            x = z[i] - rc[i]
            if carry is not None:
                x = x - carry[i]
            x = jnp.exp2(x)
            if causal is not None:
                x = jnp.where(causal, x, 0.0)
            w[i] = x.astype(BF16)

        def values(i):
            p = heads[i][0]
            pv[i] = _dot(w[i], v_ref[0, rows, p * LANES:(p + 1) * LANES])

        _emit_skewed(n, (scores, keep_terms, cumulate, weights, values))
        return pv, [r[:, 0:1] for r in rc]

    carry = None
    for diag in reversed(range(qsub)):
        pv, tot = block(qi * qsub + diag, carry, diag)
        for i, x in enumerate(pv):
            if carry is None:
                acc_ref[i] = x
            else:
                acc_ref[i] += x
        carry = tot if carry is None else [c + t for c, t in zip(carry, tot)]

    def step(it, carry):
        pv, tot = block(qi * qsub - 1 - it, carry, None)
        for i, x in enumerate(pv):
            acc_ref[i] += x
        return tuple(c + t for c, t in zip(carry, tot))

    lax.fori_loop(0, qi * qsub, step, tuple(carry))
    for p in range(n_pairs):
        o_ref[0, :, p * LANES:(p + 1) * LANES] = jnp.where(first, acc_ref[per * p], acc_ref[per * p + 1])


def _sb_attention(q, kv, blk, qsub, n_pairs):
    b, t, _ = q.shape
    width = n_pairs * LANES
    groups = SB_W // width
    tq = qsub * blk
    return pl.pallas_call(
        functools.partial(_sb_body, blk=blk, qsub=qsub, n_pairs=n_pairs),
        out_shape=jax.ShapeDtypeStruct((b, t, SB_W), F32),
        grid=(b, groups, t // tq),
        in_specs=[
            pl.BlockSpec((1, tq, width), lambda bi, g, qi: (bi, qi, g)),
            pl.BlockSpec((1, t, width), lambda bi, g, qi: (bi, 0, g)),
            pl.BlockSpec((1, t, width), lambda bi, g, qi: (bi, 0, groups + g)),
        ],
        out_specs=pl.BlockSpec((1, tq, width), lambda bi, g, qi: (bi, qi, g)),
        scratch_shapes=[pltpu.VMEM((2 * n_pairs, tq, LANES), F32)],
        compiler_params=_params(("parallel", "parallel", "arbitrary")),
        name="sb_attention",
    )(q, kv, kv)


def _half_rms_scale(x, first):
    sq = x * x
    ms0 = jnp.sum(jnp.where(first, sq, 0.0), axis=-1, keepdims=True) * (1.0 / HEAD_DIM)
    ms1 = jnp.sum(jnp.where(first, 0.0, sq), axis=-1, keepdims=True) * (1.0 / HEAD_DIM)
    return jnp.where(first, lax.rsqrt(ms0 + EPS), lax.rsqrt(ms1 + EPS))


def _swa_body(sink_ref, q_ref, kvc_ref, kvp_ref, qg_ref, kg_ref, o_ref, *, n_blocks, slopes):
    tile = pl.program_id(1)
    w = WINDOW
    group = SWA_HEADS // SWA_KV_HEADS
    per = LANES // HEAD_DIM
    assert per == 2 and SWA_KVW == LANES
    first = lax.broadcasted_iota(jnp.int32, (1, LANES), 1) < HEAD_DIM

    def both_halves(x):
        swapped = pltpu.roll(x, HEAD_DIM, axis=1)
        return jnp.where(first, x, swapped), jnp.where(first, swapped, x)

    kv = jnp.concatenate([kvp_ref[0], kvc_ref[0]], axis=0)
    k = kv[:, :SWA_KVW]
    k = k * _half_rms_scale(k, first) * kg_ref[...]
    k_dup = [x.astype(BF16) for x in both_halves(k)]
    v_dup = [x.astype(BF16) for x in both_halves(kv[:, SWA_KVW:])]

    q_heads = []
    for g2 in range(SWA_QW // LANES):
        q = q_ref[0, :, g2 * LANES:(g2 + 1) * LANES]
        q = q * _half_rms_scale(q, first) * (qg_ref[...] * (HEAD_DIM ** -0.5 * LOG2E))
        q_heads.append(jnp.where(first, q, 0.0).astype(BF16))
        q_heads.append(jnp.where(first, 0.0, q).astype(BF16))

    qpos = lax.broadcasted_iota(jnp.int32, (w, 2 * w), 0)
    kpos = lax.broadcasted_iota(jnp.int32, (w, 2 * w), 1)
    dist = qpos + w - kpos
    band = (dist >= 0) & (dist < w)
    distf = dist.astype(F32)
    has_prev = (tile > 0) | (kpos >= w)
    bias, sink = [], []
    for kvh in range(SWA_KV_HEADS):
        heads = range(kvh * group, (kvh + 1) * group)
        bias.append(jnp.concatenate(
            [jnp.where(band, (-slopes[h] * LOG2E) * distf, -jnp.inf) for h in heads], axis=0))
        sink.append(jnp.concatenate(
            [jnp.full((w, 1), sink_ref[h] * LOG2E, F32) for h in heads], axis=0))

    units = [(j, kvh) for j in range(n_blocks) for kvh in range(SWA_KV_HEADS)]
    n = len(units)
    s, e, denom, res = [None] * n, [None] * n, [None] * n, [None] * n

    def scores(i):
        j, kvh = units[i]
        q4 = jnp.concatenate([q_heads[kvh * group + g][j * w:(j + 1) * w] for g in range(group)], axis=0)
        s[i] = _dot_nt(q4, k_dup[kvh][j * w:(j + 2) * w]) + bias[kvh]
        if j == 0:
            s[i] = jnp.where(jnp.concatenate([has_prev] * group, axis=0), s[i], -jnp.inf)

    def softmax(i):
        kvh = units[i][1]
        m = jnp.maximum(jnp.max(s[i], axis=-1, keepdims=True), sink[kvh])
        p = jnp.exp2(s[i] - m)
        denom[i] = jnp.sum(p, axis=-1, keepdims=True) + jnp.exp2(sink[kvh] - m)
        e[i] = p.astype(BF16)

    def values(i):
        j, kvh = units[i]
        res[i] = _dot(e[i], v_dup[kvh][j * w:(j + 2) * w]) / denom[i]

    def store(i):
        j, kvh = units[i]
        for g2 in range(group // per):
            piece = [res[i][(per * g2 + h) * w:(per * g2 + h + 1) * w] for h in range(per)]
            lanes = (kvh * (group // per) + g2) * LANES
            o_ref[0, j * w:(j + 1) * w, lanes:lanes + LANES] = jnp.where(first, piece[0], piece[1])

    _emit_skewed(n, (scores, softmax, values, store))


def _swa_attention(proj, q_gain, k_gain, sinks, n_blocks):
    b, t, _ = proj.shape
    w = WINDOW
    tq = n_blocks * w
    kv_blk = SWA_QW // (2 * SWA_KVW)
    slopes = tuple(float(2.0 ** (-8.0 * (i + 1) / SWA_HEADS)) for i in range(SWA_HEADS))
    gain2 = lambda g: jnp.concatenate([g, g]).reshape(1, LANES)
    return pl.pallas_call(
        functools.partial(_swa_body, n_blocks=n_blocks, slopes=slopes),
        out_shape=jax.ShapeDtypeStruct((b, t, SWA_QW), F32),
        grid=(b, t // tq),
        in_specs=[
            pl.BlockSpec(memory_space=pltpu.SMEM),
            pl.BlockSpec((1, tq, SWA_QW), lambda bi, i: (bi, i, 0)),
            pl.BlockSpec((1, tq, 2 * SWA_KVW), lambda bi, i: (bi, i, kv_blk)),
            pl.BlockSpec((1, w, 2 * SWA_KVW), lambda bi, i: (bi, jnp.maximum(i * n_blocks - 1, 0), kv_blk)),
            pl.BlockSpec((1, LANES), lambda bi, i: (0, 0)),
            pl.BlockSpec((1, LANES), lambda bi, i: (0, 0)),
        ],
        out_specs=pl.BlockSpec((1, tq, SWA_QW), lambda bi, i: (bi, i, 0)),
        compiler_params=_params(("parallel", "parallel")),
        name="swa_attention",
    )(sinks, proj, proj, proj, gain2(q_gain), gain2(k_gain))


def _gdn_in_body(x_ref, halo_ref, g_ref, w_ref, wba_ref, cw_ref, alog_ref, dtb_ref,
                 qkv_ref, z_ref, gb_ref, xn_ref, pre_ref, *, tm, cols, sub):
    ti = pl.program_id(1)
    step = pl.program_id(2)
    conv_steps = GDN_CONV_W // cols
    qk_steps = GDN_KW // cols
    nh = GDN_V_HEADS

    @pl.when(step == 0)
    def _():
        xh = halo_ref[0]
        x = x_ref[0]
        xn_ref[0:CONV_HALO, :] = (xh * _rms_scale(xh) * g_ref[...]).astype(BF16)
        xn_ref[CONV_HALO:, :] = (x * _rms_scale(x) * g_ref[...]).astype(BF16)
        ba = _dot(xn_ref[CONV_HALO:, :], wba_ref[...])
        beta = _sigmoid(ba[:, :nh])
        g = -jnp.exp(alog_ref[...]) * _softplus(ba[:, nh:2 * nh] + dtb_ref[...])
        r = lax.broadcasted_iota(jnp.int32, (GDN_CHUNK, GDN_CHUNK), 0)
        c = lax.broadcasted_iota(jnp.int32, (GDN_CHUNK, GDN_CHUNK), 1)
        lower = jnp.where(r >= c, 1.0, 0.0).astype(F32)
        for chunk in range(tm // GDN_CHUNK):
            rows = slice(chunk * GDN_CHUNK, (chunk + 1) * GDN_CHUNK)
            gb_ref[0, rows, :] = jnp.concatenate([_dot_f32(lower, g[rows]), beta[rows]], axis=-1)

    @pl.when(step < conv_steps)
    def _():
        is_q = step < qk_steps
        is_k = (step >= qk_steps) & (step < 2 * qk_steps)

        def project(i):
            pre = _dot(xn_ref[...], w_ref[:, i * sub:(i + 1) * sub])
            pre_ref[i, 0:CONV_HALO, :] = jnp.where(ti > 0, pre[:CONV_HALO], 0.0)
            pre_ref[i, CONV_HALO:, :] = pre[CONV_HALO:]

        def conv(i):
            ext = pre_ref[i]
            y = None
            for tap in range(GDN_CONV):
                back = GDN_CONV - 1 - tap
                rows = ext if back == 0 else pltpu.roll(ext, back, axis=0)
                term = rows[CONV_HALO:] * cw_ref[tap:tap + 1, i * sub:(i + 1) * sub]
                y = term if y is None else y + term
            y = y / (1.0 + jnp.exp2(y * (-LOG2E)))
            for head in range(sub // GDN_HEAD_DIM):
                lanes = slice(head * GDN_HEAD_DIM, (head + 1) * GDN_HEAD_DIM)
                yh = y[:, lanes]
                inv = lax.rsqrt(jnp.sum(yh * yh, axis=-1, keepdims=True) + EPS)
                mult = jnp.where(is_q, inv * (GDN_HEAD_DIM ** -0.5), jnp.where(is_k, inv, 1.0))
                qkv_ref[0, :, i * sub + head * GDN_HEAD_DIM:i * sub + (head + 1) * GDN_HEAD_DIM] = yh * mult

        _emit_skewed(cols // sub, (project, conv))

    @pl.when(step >= conv_steps)
    def _():
        z_ref[0] = _dot(xn_ref[CONV_HALO:, :], w_ref[...])


def _gdn_in(h3, gain, w, conv_w, a_log, dt_bias, tm):
    b, t, d = h3.shape
    nh = GDN_V_HEADS
    cols = GDN_IN_COLS
    conv_steps = GDN_CONV_W // cols
    steps = (GDN_CONV_W + GDN_VW) // cols
    w_ba = w[:, GDN_CONV_W + GDN_VW:]
    return pl.pallas_call(
        functools.partial(_gdn_in_body, tm=tm, cols=cols, sub=GDN_IN_SUB),
        out_shape=[
            jax.ShapeDtypeStruct((b, t, GDN_CONV_W), F32),
            jax.ShapeDtypeStruct((b, t, GDN_VW), F32),
            jax.ShapeDtypeStruct((b, t, 2 * nh), F32),
        ],
        grid=(b, t // tm, steps),
        in_specs=[
            pl.BlockSpec((1, tm, d), lambda bi, ti, s: (bi, ti, 0)),
            pl.BlockSpec((1, CONV_HALO, d),
                         lambda bi, ti, s: (bi, jnp.maximum(ti * (tm // CONV_HALO) - 1, 0), 0)),
            pl.BlockSpec((1, d), lambda bi, ti, s: (0, 0)),
            pl.BlockSpec((d, cols), lambda bi, ti, s: (0, s)),
            pl.BlockSpec(w_ba.shape, lambda bi, ti, s: (0, 0)),
            pl.BlockSpec((GDN_CONV, cols), lambda bi, ti, s: (0, jnp.minimum(s, conv_steps - 1))),
            pl.BlockSpec((1, nh), lambda bi, ti, s: (0, 0)),
            pl.BlockSpec((1, nh), lambda bi, ti, s: (0, 0)),
        ],
        out_specs=[
            pl.BlockSpec((1, tm, cols), lambda bi, ti, s: (bi, ti, jnp.minimum(s, conv_steps - 1))),
            pl.BlockSpec((1, tm, cols), lambda bi, ti, s: (bi, ti, jnp.maximum(s - conv_steps, 0))),
            pl.BlockSpec((1, tm, 2 * nh), lambda bi, ti, s: (bi, ti, 0)),
        ],
        scratch_shapes=[
            pltpu.VMEM((CONV_HALO + tm, d), BF16),
            pltpu.VMEM((cols // GDN_IN_SUB, CONV_HALO + tm, GDN_IN_SUB), F32),
        ],
        compiler_params=_params(("parallel", "parallel", "arbitrary")),
        name="gdn_in",
    )(h3, h3, gain.reshape(1, d), w, w_ba, conv_w, a_log.reshape(1, nh), dt_bias.reshape(1, nh))


def _split(x):
    hi = x.astype(BF16)
    return hi, (x - hi.astype(F32)).astype(BF16)


def _block_diag(y, mask):
    return jnp.where(mask, jnp.concatenate([y, y], axis=0), jnp.zeros((), y.dtype))


def _pair_lhs(x):
    hi, lo = _split(x)
    return jnp.concatenate([hi, hi, lo], axis=1)


def _pair_rhs(y, mask):
    hi, lo = _split(y)
    hi, lo = _block_diag(hi, mask), _block_diag(lo, mask)
    return jnp.concatenate([hi, lo, hi], axis=0)


def _pair_mm(lhs, rhs):
    return _dot(lhs, rhs)


def _unit_lower_inverse_pairs(a_list, eye, same_block, bd_mask):
    mm = lambda xs, ys: [_pair_mm(_pair_lhs(x), _pair_rhs(y, bd_mask)) for x, y in zip(xs, ys)]
    add = lambda xs, ys: [x + y for x, y in zip(xs, ys)]
    c = a_list[0].shape[0]
    ad = [jnp.where(same_block, a, 0.0) for a in a_list]
    lo = [a - d for a, d in zip(a_list, ad)]
    dinv = [eye - d for d in ad]
    power = ad
    span = 2
    while span < INV_BLOCK:
        power = mm(power, power)
        dinv = add(dinv, mm(dinv, power))
        span *= 2
    n = mm(dinv, lo)
    m = [eye - x for x in n]
    power = n
    span = 2
    while span < c // INV_BLOCK:
        power = mm(power, power)
        m = add(m, mm(m, power))
        span *= 2
    return mm(m, dinv)


def _gdn_local_body(q_ref, k_ref, v_ref, gb_ref, gr_ref, u_ref, wq_ref, kt_ref, at_ref, *, n_chunks):
    pair = pl.program_id(1)
    cs = GDN_CHUNK
    nh = GDN_V_HEADS
    d = GDN_HEAD_DIM
    rep = GDN_V_HEADS // GDN_K_HEADS
    assert rep == 2 and rep * cs == LANES

    ii = lax.broadcasted_iota(jnp.int32, (cs, rep * cs), 0)
    lane = lax.broadcasted_iota(jnp.int32, (cs, rep * cs), 1)
    jj = lane % cs
    first = lane < cs
    eye = jnp.where(ii == jj, 1.0, 0.0).astype(F32)
    same_block = (ii // INV_BLOCK) == (jj // INV_BLOCK)
    r2 = lax.broadcasted_iota(jnp.int32, (rep * cs, rep * cs), 0)
    c2 = lax.broadcasted_iota(jnp.int32, (rep * cs, rep * cs), 1)
    bd_mask = (r2 // cs) == (c2 // cs)
    lane_gb = lax.broadcasted_iota(jnp.int32, (cs, 2 * nh), 1)
    zeros_rhs = jnp.zeros((cs, 2 * d), BF16)
    chunks = range(n_chunks)
    rows = [slice(c * cs, (c + 1) * cs) for c in chunks]

    def column(gb, idx):
        return jnp.sum(jnp.where(lane_gb == idx, gb, 0.0), axis=-1, keepdims=True)

    q = [q_ref[0, r, :] for r in rows]
    k = [k_ref[0, r, :] for r in rows]
    gb = [gb_ref[0, r, :] for r in rows]
    gc = [[column(g, rep * pair + h) for h in range(rep)] for g in gb]
    beta = [[column(g, nh + rep * pair + h) for h in range(rep)] for g in gb]
    gc_row = [gr_ref[0, 0, c] for c in chunks]
    decay = [jnp.exp(jnp.where(ii >= jj, jnp.where(first, gc[c][0], gc[c][1]) - gc_row[c], -jnp.inf))
             for c in chunks]
    k16 = [x.astype(BF16) for x in k]
    kk16 = [jnp.concatenate([x, x], axis=0) for x in k16]
    a = [jnp.where(ii > jj, _dot_nt(k16[c], kk16[c]) * jnp.where(first, beta[c][0], beta[c][1]) * decay[c], 0.0)
         for c in chunks]
    for c in chunks:
        at_ref[0, 0, rows[c], :] = jnp.where(
            ii >= jj, _dot_nt(q[c].astype(BF16), kk16[c]) * decay[c], 0.0).astype(BF16)
    tinv = _unit_lower_inverse_pairs(a, eye, same_block, bd_mask)

    eg = [[jnp.exp(gc[c][h]) for h in range(rep)] for c in chunks]
    sol = []
    for c in chunks:
        rhs_hi, rhs_lo = [], []
        for h in range(rep):
            v = v_ref[0, rows[c], h * d:(h + 1) * d]
            hi, lo = _split(jnp.concatenate([v * beta[c][h], k[c] * (beta[c][h] * eg[c][h])], axis=-1))
            pad = [zeros_rhs] * rep
            rhs_hi.append(jnp.concatenate(pad[:h] + [hi] + pad[h + 1:], axis=1))
            rhs_lo.append(jnp.concatenate(pad[:h] + [lo] + pad[h + 1:], axis=1))
        rhs_hi = jnp.concatenate(rhs_hi, axis=0)
        rhs_lo = jnp.concatenate(rhs_lo, axis=0)
        sol.append(_dot(_pair_lhs(tinv[c]), jnp.concatenate([rhs_hi, rhs_lo, rhs_hi], axis=0)))

    for c in chunks:
        for h in range(rep):
            u_ref[0, rows[c], h * d:(h + 1) * d] = sol[c][:, 2 * d * h:2 * d * h + d]
            wq_ref[0, h, c, :cs, :] = sol[c][:, 2 * d * h + d:2 * d * (h + 1)].astype(BF16)
            wq_ref[0, h, c, cs:, :] = (q[c] * eg[c][h]).astype(BF16)
            g_last = gc_row[c][:, h * cs + cs - 1:(h + 1) * cs]
            kt_ref[0, h, c] = (k[c] * jnp.exp(g_last - gc[c][h])).T.astype(BF16)


def _gdn_local(qkv, gb, gc_rows, n_chunks):
    b, t, _ = qkv.shape
    nh = GDN_V_HEADS
    rep = GDN_V_HEADS // GDN_K_HEADS
    cs = GDN_CHUNK
    tt = n_chunks * cs
    d = GDN_HEAD_DIM
    return pl.pallas_call(
        functools.partial(_gdn_local_body, n_chunks=n_chunks),
        out_shape=[
            jax.ShapeDtypeStruct((b, t, GDN_VW), F32),
            jax.ShapeDtypeStruct((b, nh, t // cs, 2 * cs, d), BF16),
            jax.ShapeDtypeStruct((b, nh, t // cs, d, cs), BF16),
            jax.ShapeDtypeStruct((b, GDN_K_HEADS, t, rep * cs), BF16),
        ],
        grid=(b, GDN_K_HEADS, t // tt),
        in_specs=[
            pl.BlockSpec((1, tt, d), lambda bi, p, ti: (bi, ti, p)),
            pl.BlockSpec((1, tt, d), lambda bi, p, ti: (bi, ti, GDN_K_HEADS + p)),
            pl.BlockSpec((1, tt, rep * d), lambda bi, p, ti: (bi, ti, GDN_K_HEADS + p)),
            pl.BlockSpec((1, tt, 2 * nh), lambda bi, p, ti: (bi, ti, 0)),
            pl.BlockSpec((1, 1, n_chunks, 1, rep * cs), lambda bi, p, ti: (bi, p, ti, 0, 0)),
        ],
        out_specs=[
            pl.BlockSpec((1, tt, rep * d), lambda bi, p, ti: (bi, ti, p)),
            pl.BlockSpec((1, rep, n_chunks, 2 * cs, d), lambda bi, p, ti: (bi, p, ti, 0, 0)),
            pl.BlockSpec((1, rep, n_chunks, d, cs), lambda bi, p, ti: (bi, p, ti, 0, 0)),
            pl.BlockSpec((1, 1, tt, rep * cs), lambda bi, p, ti: (bi, p, ti, 0)),
        ],
        compiler_params=_params(("parallel", "parallel", "parallel")),
        name="gdn_local",
    )(qkv, qkv, qkv, gb, gc_rows)


def _gdn_scan_body(u_ref, wq_ref, kt_ref, at_ref, gr_ref, z_ref, g_ref, y_ref, s_ref, *, n_chunks, n_pairs):
    cs = GDN_CHUNK
    d = GDN_HEAD_DIM
    rep = GDN_V_HEADS // GDN_K_HEADS
    heads = range(rep * n_pairs)

    @pl.when(pl.program_id(2) == 0)
    def _():
        s_ref[...] = jnp.zeros_like(s_ref)

    zeros_v = jnp.zeros((cs, d), BF16)
    states = [s_ref[i] for i in heads]
    for c in range(n_chunks):
        rows = slice(c * cs, (c + 1) * cs)
        s16 = [s.astype(BF16) for s in states]
        both = [_dot(wq_ref[0, i, c], s16[i]) for i in heads]
        v_new = [(u_ref[0, rows, i * d:(i + 1) * d] - both[i][:cs]).astype(BF16) for i in heads]
        for p in range(n_pairs):
            gc_row = gr_ref[0, p, c]
            diag = []
            for h in range(rep):
                pad = [zeros_v] * rep
                diag.append(jnp.concatenate(pad[:h] + [v_new[rep * p + h]] + pad[h + 1:], axis=1))
            o_attn = _dot(at_ref[0, p, rows, :], jnp.concatenate(diag, axis=0))
            for h in range(rep):
                i = rep * p + h
                o = both[i][cs:] + o_attn[:, h * d:(h + 1) * d]
                z = z_ref[0, rows, i * d:(i + 1) * d]
                gate = z / (1.0 + jnp.exp2(z * (-LOG2E)))
                y_ref[0, rows, i * d:(i + 1) * d] = (o * _rms_scale(o) * g_ref[...] * gate).astype(BF16)
                g_last = gc_row[:, h * cs + cs - 1:(h + 1) * cs]
                states[i] = states[i] * jnp.exp(g_last) + _dot(kt_ref[0, i, c], v_new[i])
    for i in heads:
        s_ref[i] = states[i]


def _gdn_scan(u, wq, kt, attn, gc_rows, z, gain, n_chunks, n_pairs):
    b, t, _ = u.shape
    rep = GDN_V_HEADS // GDN_K_HEADS
    cs = GDN_CHUNK
    tt = n_chunks * cs
    d = GDN_HEAD_DIM
    nhs = rep * n_pairs
    return pl.pallas_call(
        functools.partial(_gdn_scan_body, n_chunks=n_chunks, n_pairs=n_pairs),
        out_shape=jax.ShapeDtypeStruct((b, t, GDN_VW), BF16),
        grid=(b, GDN_K_HEADS // n_pairs, t // tt),
        in_specs=[
            pl.BlockSpec((1, tt, nhs * d), lambda bi, g, ti: (bi, ti, g)),
            pl.BlockSpec((1, nhs, n_chunks, 2 * cs, d), lambda bi, g, ti: (bi, g, ti, 0, 0)),
            pl.BlockSpec((1, nhs, n_chunks, d, cs), lambda bi, g, ti: (bi, g, ti, 0, 0)),
            pl.BlockSpec((1, n_pairs, tt, rep * cs), lambda bi, g, ti: (bi, g, ti, 0)),
            pl.BlockSpec((1, n_pairs, n_chunks, 1, rep * cs), lambda bi, g, ti: (bi, g, ti, 0, 0)),
            pl.BlockSpec((1, tt, nhs * d), lambda bi, g, ti: (bi, ti, g)),
            pl.BlockSpec((1, d), lambda bi, g, ti: (0, 0)),
        ],
        out_specs=pl.BlockSpec((1, tt, nhs * d), lambda bi, g, ti: (bi, ti, g)),
        scratch_shapes=[pltpu.VMEM((nhs, d, d), F32)],
        compiler_params=_params(("parallel", "parallel", "arbitrary")),
        name="gdn_scan",
    )(u, wq, kt, attn, gc_rows, z, gain.reshape(1, d))


def _attention_mixer(h, b, t, gain, w_in, q_gain, k_gain, sinks, w_out, tm):
    d = h.shape[1]
    swa_w = SWA_QW + 2 * SWA_KVW
    outs = ((SB_W, BF16, HEAD_DIM ** -0.5 * LOG2E), (2 * SB_W, BF16, None), (swa_w, F32, None))
    sb_q, sb_kv, swa = _norm_proj(h, gain, w_in.astype(BF16), outs, tm)
    a_out = _sb_attention(sb_q.reshape(b, t, SB_W), sb_kv.reshape(b, t, 2 * SB_W),
                          min(SB_BLOCK, t), SB_Q_BLOCKS if t >= SB_Q_BLOCKS * SB_BLOCK else 1,
                          SB_PAIRS).reshape(b * t, SB_W)
    b_out = _swa_attention(swa.reshape(b, t, swa_w), q_gain, k_gain, sinks,
                           min(SWA_TILE_BLOCKS, t // WINDOW)).reshape(b * t, SWA_QW)
    return _mixer_out(h, [a_out, b_out], w_out.astype(BF16), tm)


def _gdn_mixer(h, b, t, gain, w_in, conv_w, a_log, dt_bias, out_gain, w_out, tm):
    nh = GDN_V_HEADS
    ba_w = w_in.shape[1] - GDN_CONV_W - GDN_VW
    assert ba_w == 2 * nh
    qkv, z, gb = _gdn_in(h.reshape(b, t, -1), gain, w_in.astype(BF16), conv_w, a_log, dt_bias,
                         min(GDN_IN_TILE, t))
    rep = GDN_V_HEADS // GDN_K_HEADS
    gc_rows = gb[:, :, :nh].reshape(b, t // GDN_CHUNK, GDN_CHUNK, GDN_K_HEADS, rep)
    gc_rows = jnp.transpose(gc_rows, (0, 3, 1, 4, 2)).reshape(
        b, GDN_K_HEADS, t // GDN_CHUNK, 1, rep * GDN_CHUNK)
    u, wq, kt, attn = _gdn_local(qkv, gb, gc_rows, min(GDN_LOCAL_CHUNKS, t // GDN_CHUNK))
    y = _gdn_scan(u, wq, kt, attn, gc_rows, z, out_gain, min(GDN_SCAN_CHUNKS, t // GDN_CHUNK), GDN_SCAN_PAIRS)
    return _mixer_out(h, [y.reshape(b * t, GDN_VW)], w_out.astype(BF16), tm)


def kernel(x, p, ffn_norm, ffn_w_gate, ffn_w_up, ffn_w_down, mix_norm, att_w_in, att_q_norm, att_k_norm, att_sinks, att_w_out, gdn_w_in, gdn_conv_w, gdn_a_log, gdn_dt_bias, gdn_out_norm, gdn_w_out, ple_norm, ple_w_gate, ple_w_proj):
    b, t, d = x.shape
    n = b * t
    tm = min(TOKEN_TILE, n)
    depth = p.shape[0]
    h = x.reshape(n, d)
    wg, wu, wd = ffn_w_gate.astype(BF16), ffn_w_up.astype(BF16), ffn_w_down.astype(BF16)
    for i in range(depth):
        j = i // 2
        h = _ffn(h, ffn_norm[i, 0], wg, wu, wd, i, 0, tm)
        if i % 2 == 0:
            h = _attention_mixer(h, b, t, mix_norm[i], att_w_in[j], att_q_norm[j], att_k_norm[j],
                                 att_sinks[j], att_w_out[j], tm)
        else:
            h = _gdn_mixer(h, b, t, mix_norm[i], gdn_w_in[j], gdn_conv_w[j], gdn_a_log[j],
                           gdn_dt_bias[j], gdn_out_norm[j], gdn_w_out[j], tm)
        h = _ffn(h, ffn_norm[i, 1], wg, wu, wd, i, 1, tm)
        h = _ple(h, p.reshape(depth, n, -1), i, ple_norm[i], ple_w_gate[i].astype(BF16),
                 ple_w_proj[i].astype(BF16), tm)
    return h.reshape(b, t, d)
```

```python
import functools

import jax
import jax.numpy as jnp
from jax import lax
from jax.experimental import pallas as pl
from jax.experimental.pallas import tpu as pltpu

F32 = jnp.float32
BF16 = jnp.bfloat16

EPS = 1e-6
LOG2E = 1.4426950408889634
HEAD_DIM = 64
SB_HEADS = 8
SWA_HEADS = 8
SWA_KV_HEADS = 2
WINDOW = 128
GDN_K_HEADS = 8
GDN_V_HEADS = 16
GDN_HEAD_DIM = 128
GDN_CONV = 4
GDN_CHUNK = 64
SB_W = SB_HEADS * HEAD_DIM
SWA_QW = SWA_HEADS * HEAD_DIM
SWA_KVW = SWA_KV_HEADS * HEAD_DIM
GDN_KW = GDN_K_HEADS * GDN_HEAD_DIM
GDN_VW = GDN_V_HEADS * GDN_HEAD_DIM
GDN_CONV_W = 2 * GDN_KW + GDN_VW

LANES = 128
VMEM_LIMIT = 56 * 1024 * 1024

TOKEN_TILE = 512
FF_CHUNK = 256
SB_BLOCK = 256
SB_Q_BLOCKS = 2
SB_CUMSUM_TERMS = 1
SB_PAIRS = 4
SWA_TILE_BLOCKS = 4
GDN_IN_TILE = 1024
GDN_IN_COLS = 1024
GDN_IN_SUB = 256
CONV_HALO = 16
GDN_LOCAL_CHUNKS = 8
GDN_SCAN_CHUNKS = 8
GDN_SCAN_PAIRS = 4
INV_BLOCK = 16


def _params(sem):
    return pltpu.CompilerParams(dimension_semantics=sem, vmem_limit_bytes=VMEM_LIMIT)


def _dot(a, b):
    return jnp.dot(a, b, preferred_element_type=F32)


def _dot_nt(a, b):
    return lax.dot_general(a, b, (((1,), (1,)), ((), ())), preferred_element_type=F32)


def _dot_f32(a, b):
    return jnp.dot(a, b, preferred_element_type=F32, precision=lax.Precision.HIGHEST)


def _rms_scale(x):
    return lax.rsqrt(jnp.mean(x * x, axis=-1, keepdims=True) + EPS)


def _sigmoid(x):
    return 1.0 / (1.0 + jnp.exp(-x))


def _emit_skewed(n, stages):
    for tick in range(n + len(stages) - 1):
        for s, stage in enumerate(stages):
            if 0 <= tick - s < n:
                stage(tick - s)


def _neg_abs(x):
    bits = lax.bitcast_convert_type(x, jnp.uint32) | jnp.uint32(0x80000000)
    return lax.bitcast_convert_type(bits, F32)


def _softplus(x):
    return jnp.maximum(x, 0.0) + jnp.log1p(jnp.exp(-jnp.abs(x)))


def _ffn_body(*refs, n_chunks, fc, n_mix, ple):
    refs = list(refs)
    x_ref, g_ref, wg_ref, wu_ref, wd_ref = refs[:5]
    refs = refs[5:]
    mix_refs, refs = refs[:n_mix], refs[n_mix:]
    if n_mix:
        wmix_ref, refs = refs[0], refs[1:]
    if ple:
        p_ref, pg_ref, pwg_ref, pwp_ref = refs[:4]
        refs = refs[4:]
    o_ref, acc_ref = refs

    x = x_ref[...]
    start = 0
    for m_ref in mix_refs:
        width = m_ref.shape[1]
        x = x + _dot(m_ref[...].astype(BF16), wmix_ref[start:start + width, :])
        start += width
    xn = (x * _rms_scale(x) * g_ref[...]).astype(BF16)
    for c in range(n_chunks):
        cols = slice(c * fc, (c + 1) * fc)
        gate = _dot(xn, wg_ref[:, cols])
        up = _dot(xn, wu_ref[:, cols])
        act = (gate * _sigmoid(gate) * up).astype(BF16)
        part = _dot(act, wd_ref[cols, :])
        if c == 0:
            acc_ref[...] = part
        else:
            acc_ref[...] += part
    y = x + 0.5 * acc_ref[...]
    if ple:
        yn = (y * _rms_scale(y) * pg_ref[...]).astype(BF16)
        y = y + _sigmoid(_dot(yn, pwg_ref[...])) * _dot(p_ref[0].astype(BF16), pwp_ref[...])
    o_ref[...] = y


def _ffn(h, gain, wg, wu, wd, layer, half, tm, mix=None, ple=None):
    n, d = h.shape
    f = wg.shape[-1]
    fc = FF_CHUNK if f % FF_CHUNK == 0 else f
    const = lambda i: (0, 0)
    row = lambda i: (i, 0)
    pick = lambda i: (layer, half, 0, 0)
    whole = lambda a: pl.BlockSpec(a.shape, const, pipeline_mode=pl.Buffered(1))
    args = [h, gain.reshape(1, d), wg, wu, wd]
    in_specs = [
        pl.BlockSpec((tm, d), row),
        pl.BlockSpec((1, d), const),
        pl.BlockSpec((None, None, d, f), pick, pipeline_mode=pl.Buffered(1)),
        pl.BlockSpec((None, None, d, f), pick, pipeline_mode=pl.Buffered(1)),
        pl.BlockSpec((None, None, f, d), pick, pipeline_mode=pl.Buffered(1)),
    ]
    n_mix = 0
    if mix is not None:
        xs, w_mix = mix
        assert sum(x.shape[1] for x in xs) == w_mix.shape[0]
        n_mix = len(xs)
        args += list(xs) + [w_mix]
        in_specs += [pl.BlockSpec((tm, x.shape[1]), row) for x in xs] + [whole(w_mix)]
    if ple is not None:
        p, p_gain, w_gate, w_proj = ple
        args += [p, p_gain.reshape(1, d), w_gate, w_proj]
        in_specs += [pl.BlockSpec((1, tm, p.shape[2]), lambda i: (layer, i, 0)),
                     pl.BlockSpec((1, d), const), whole(w_gate), whole(w_proj)]
    return pl.pallas_call(
        functools.partial(_ffn_body, n_chunks=f // fc, fc=fc, n_mix=n_mix, ple=ple is not None),
        out_shape=jax.ShapeDtypeStruct((n, d), F32),
        grid=(n // tm,),
        in_specs=in_specs,
        out_specs=pl.BlockSpec((tm, d), row),
        scratch_shapes=[pltpu.VMEM((tm, d), F32)],
        compiler_params=_params(("parallel",)),
        name="ffn",
    )(*args)


def _norm_proj_body(x_ref, g_ref, w_ref, *o_refs, outs):
    x = x_ref[...]
    xn = (x * _rms_scale(x) * g_ref[...]).astype(BF16)
    start = 0
    for o_ref, (width, _, scale) in zip(o_refs, outs):
        y = _dot(xn, w_ref[:, start:start + width])
        o_ref[...] = (y if scale is None else y * scale).astype(o_ref.dtype)
        start += width


def _norm_proj(h, gain, w, outs, tm):
    n, d = h.shape
    widths = [o[0] for o in outs]
    assert sum(widths) == w.shape[1]
    return pl.pallas_call(
        functools.partial(_norm_proj_body, outs=outs),
        out_shape=[jax.ShapeDtypeStruct((n, width), dtype) for width, dtype, _ in outs],
        grid=(n // tm,),
        in_specs=[
            pl.BlockSpec((tm, d), lambda i: (i, 0)),
            pl.BlockSpec((1, d), lambda i: (0, 0)),
            pl.BlockSpec(w.shape, lambda i: (0, 0), pipeline_mode=pl.Buffered(1)),
        ],
        out_specs=[pl.BlockSpec((tm, width), lambda i: (i, 0)) for width in widths],
        compiler_params=_params(("parallel",)),
        name="norm_proj",
    )(h, gain.reshape(1, d), w)


def _sb_body(q_ref, k_ref, v_ref, o_ref, acc_ref, *, blk, qsub, n_pairs):
    qi = pl.program_id(2)
    per = LANES // HEAD_DIM
    assert per == 2
    tq = qsub * blk
    row = lax.broadcasted_iota(jnp.int32, (tq, blk), 0)
    col = lax.broadcasted_iota(jnp.int32, (tq, blk), 1)
    later = jnp.where(lax.broadcasted_iota(jnp.int32, (blk, blk), 0)
                      >= lax.broadcasted_iota(jnp.int32, (blk, blk), 1), 1.0, 0.0).astype(BF16)
    later2 = jnp.concatenate([later, later], axis=0)
    first = lax.broadcasted_iota(jnp.int32, (tq, LANES), 1) < HEAD_DIM
    heads = [(p, h) for p in range(n_pairs) for h in range(per)]
    q = []
    for p, h in heads:
        qp = q_ref[0, :, p * LANES:(p + 1) * LANES]
        q.append(jnp.where(first if h == 0 else ~first, qp, jnp.zeros_like(qp)))

    def block(j, carry, diag):
        start = pl.multiple_of(j * blk, blk)
        rows = pl.ds(start, blk)
        causal = None if diag is None else col + diag * blk < row
        n = len(heads)
        z, split, rc, w, pv = [None] * n, [None] * n, [None] * n, [None] * n, [None] * n

        def scores(i):
            p = heads[i][0]
            z[i] = _dot_nt(q[i], k_ref[0, rows, p * LANES:(p + 1) * LANES])

        def keep_terms(i):
            sp = jnp.maximum(z[i], 0.0) + jnp.log(1.0 + jnp.exp2(_neg_abs(z[i]))) * LOG2E
            if causal is not None:
                sp = jnp.where(causal, sp, 0.0)
            hi = sp.astype(BF16)
            if SB_CUMSUM_TERMS == 1:
                split[i] = hi
            else:
                split[i] = jnp.concatenate([hi, (sp - hi.astype(F32)).astype(BF16)], axis=1)

        def cumulate(i):
            rc[i] = _dot(split[i], later if SB_CUMSUM_TERMS == 1 else later2)

        def weights(i):
            x = z[i] - rc[i]
            if carry is not None:
                x = x - carry[i]
            x = jnp.exp2(x)
            if causal is not None:
                x = jnp.where(causal, x, 0.0)
            w[i] = x.astype(BF16)

        def values(i):
            p = heads[i][0]
            pv[i] = _dot(w[i], v_ref[0, rows, p * LANES:(p + 1) * LANES])

        _emit_skewed(n, (scores, keep_terms, cumulate, weights, values))
        return pv, [r[:, 0:1] for r in rc]

    carry = None
    for diag in reversed(range(qsub)):
        pv, tot = block(qi * qsub + diag, carry, diag)
        for i, x in enumerate(pv):
            if carry is None:
                acc_ref[i] = x
            else:
                acc_ref[i] += x
        carry = tot if carry is None else [c + t for c, t in zip(carry, tot)]

    def step(it, carry):
        pv, tot = block(qi * qsub - 1 - it, carry, None)
        for i, x in enumerate(pv):
            acc_ref[i] += x
        return tuple(c + t for c, t in zip(carry, tot))

    lax.fori_loop(0, qi * qsub, step, tuple(carry))
    for p in range(n_pairs):
        o_ref[0, :, p * LANES:(p + 1) * LANES] = jnp.where(first, acc_ref[per * p], acc_ref[per * p + 1])


def _sb_attention(q, kv, blk, qsub, n_pairs):
    b, t, _ = q.shape
    width = n_pairs * LANES
    groups = SB_W // width
    tq = qsub * blk
    return pl.pallas_call(
        functools.partial(_sb_body, blk=blk, qsub=qsub, n_pairs=n_pairs),
        out_shape=jax.ShapeDtypeStruct((b, t, SB_W), F32),
        grid=(b, groups, t // tq),
        in_specs=[
            pl.BlockSpec((1, tq, width), lambda bi, g, qi: (bi, qi, g)),
            pl.BlockSpec((1, t, width), lambda bi, g, qi: (bi, 0, g)),
            pl.BlockSpec((1, t, width), lambda bi, g, qi: (bi, 0, groups + g)),
        ],
        out_specs=pl.BlockSpec((1, tq, width), lambda bi, g, qi: (bi, qi, g)),
        scratch_shapes=[pltpu.VMEM((2 * n_pairs, tq, LANES), F32)],
        compiler_params=_params(("parallel", "parallel", "arbitrary")),
        name="sb_attention",
    )(q, kv, kv)


def _half_rms_scale(x, first):
    sq = x * x
    ms0 = jnp.sum(jnp.where(first, sq, 0.0), axis=-1, keepdims=True) * (1.0 / HEAD_DIM)
    ms1 = jnp.sum(jnp.where(first, 0.0, sq), axis=-1, keepdims=True) * (1.0 / HEAD_DIM)
    return jnp.where(first, lax.rsqrt(ms0 + EPS), lax.rsqrt(ms1 + EPS))


def _swa_body(sink_ref, q_ref, kvc_ref, kvp_ref, qg_ref, kg_ref, o_ref, *, n_blocks, slopes):
    tile = pl.program_id(1)
    w = WINDOW
    group = SWA_HEADS // SWA_KV_HEADS
    per = LANES // HEAD_DIM
    assert per == 2 and SWA_KVW == LANES
    first = lax.broadcasted_iota(jnp.int32, (1, LANES), 1) < HEAD_DIM

    def both_halves(x):
        swapped = pltpu.roll(x, HEAD_DIM, axis=1)
        return jnp.where(first, x, swapped), jnp.where(first, swapped, x)

    kv = jnp.concatenate([kvp_ref[0], kvc_ref[0]], axis=0)
    k = kv[:, :SWA_KVW]
    k = k * _half_rms_scale(k, first) * kg_ref[...]
    k_dup = [x.astype(BF16) for x in both_halves(k)]
    v_dup = [x.astype(BF16) for x in both_halves(kv[:, SWA_KVW:])]

    q_heads = []
    for g2 in range(SWA_QW // LANES):
        q = q_ref[0, :, g2 * LANES:(g2 + 1) * LANES]
        q = q * _half_rms_scale(q, first) * (qg_ref[...] * (HEAD_DIM ** -0.5 * LOG2E))
        q_heads.append(jnp.where(first, q, 0.0).astype(BF16))
        q_heads.append(jnp.where(first, 0.0, q).astype(BF16))

    qpos = lax.broadcasted_iota(jnp.int32, (w, 2 * w), 0)
    kpos = lax.broadcasted_iota(jnp.int32, (w, 2 * w), 1)
    dist = qpos + w - kpos
    band = (dist >= 0) & (dist < w)
    distf = dist.astype(F32)
    has_prev = (tile > 0) | (kpos >= w)
    bias, sink = [], []
    for kvh in range(SWA_KV_HEADS):
        heads = range(kvh * group, (kvh + 1) * group)
        bias.append(jnp.concatenate(
            [jnp.where(band, (-slopes[h] * LOG2E) * distf, -jnp.inf) for h in heads], axis=0))
        sink.append(jnp.concatenate(
            [jnp.full((w, 1), sink_ref[h] * LOG2E, F32) for h in heads], axis=0))

    units = [(j, kvh) for j in range(n_blocks) for kvh in range(SWA_KV_HEADS)]
    n = len(units)
    s, e, denom, res = [None] * n, [None] * n, [None] * n, [None] * n

    def scores(i):
        j, kvh = units[i]
        q4 = jnp.concatenate([q_heads[kvh * group + g][j * w:(j + 1) * w] for g in range(group)], axis=0)
        s[i] = _dot_nt(q4, k_dup[kvh][j * w:(j + 2) * w]) + bias[kvh]
        if j == 0:
            s[i] = jnp.where(jnp.concatenate([has_prev] * group, axis=0), s[i], -jnp.inf)

    def softmax(i):
        kvh = units[i][1]
        m = jnp.maximum(jnp.max(s[i], axis=-1, keepdims=True), sink[kvh])
        p = jnp.exp2(s[i] - m)
        denom[i] = jnp.sum(p, axis=-1, keepdims=True) + jnp.exp2(sink[kvh] - m)
        e[i] = p.astype(BF16)

    def values(i):
        j, kvh = units[i]
        res[i] = _dot(e[i], v_dup[kvh][j * w:(j + 2) * w]) / denom[i]

    def store(i):
        j, kvh = units[i]
        for g2 in range(group // per):
            piece = [res[i][(per * g2 + h) * w:(per * g2 + h + 1) * w] for h in range(per)]
            lanes = (kvh * (group // per) + g2) * LANES
            o_ref[0, j * w:(j + 1) * w, lanes:lanes + LANES] = jnp.where(first, piece[0], piece[1])

    _emit_skewed(n, (scores, softmax, values, store))


def _swa_attention(proj, q_gain, k_gain, sinks, n_blocks):
    b, t, _ = proj.shape
    w = WINDOW
    tq = n_blocks * w
    kv_blk = SWA_QW // (2 * SWA_KVW)
    slopes = tuple(float(2.0 ** (-8.0 * (i + 1) / SWA_HEADS)) for i in range(SWA_HEADS))
    gain2 = lambda g: jnp.concatenate([g, g]).reshape(1, LANES)
    return pl.pallas_call(
        functools.partial(_swa_body, n_blocks=n_blocks, slopes=slopes),
        out_shape=jax.ShapeDtypeStruct((b, t, SWA_QW), F32),
        grid=(b, t // tq),
        in_specs=[
            pl.BlockSpec(memory_space=pltpu.SMEM),
            pl.BlockSpec((1, tq, SWA_QW), lambda bi, i: (bi, i, 0)),
            pl.BlockSpec((1, tq, 2 * SWA_KVW), lambda bi, i: (bi, i, kv_blk)),
            pl.BlockSpec((1, w, 2 * SWA_KVW), lambda bi, i: (bi, jnp.maximum(i * n_blocks - 1, 0), kv_blk)),
            pl.BlockSpec((1, LANES), lambda bi, i: (0, 0)),
            pl.BlockSpec((1, LANES), lambda bi, i: (0, 0)),
        ],
        out_specs=pl.BlockSpec((1, tq, SWA_QW), lambda bi, i: (bi, i, 0)),
        compiler_params=_params(("parallel", "parallel")),
        name="swa_attention",
    )(sinks, proj, proj, proj, gain2(q_gain), gain2(k_gain))


def _gdn_in_body(x_ref, halo_ref, g_ref, w_ref, wba_ref, cw_ref, alog_ref, dtb_ref,
                 qkv_ref, z_ref, gb_ref, xn_ref, pre_ref, *, tm, cols, sub):
    ti = pl.program_id(1)
    step = pl.program_id(2)
    conv_steps = GDN_CONV_W // cols
    qk_steps = GDN_KW // cols
    nh = GDN_V_HEADS

    @pl.when(step == 0)
    def _():
        xh = halo_ref[0]
        x = x_ref[0]
        xn_ref[0:CONV_HALO, :] = (xh * _rms_scale(xh) * g_ref[...]).astype(BF16)
        xn_ref[CONV_HALO:, :] = (x * _rms_scale(x) * g_ref[...]).astype(BF16)
        ba = _dot(xn_ref[CONV_HALO:, :], wba_ref[...])
        beta = _sigmoid(ba[:, :nh])
        g = -jnp.exp(alog_ref[...]) * _softplus(ba[:, nh:2 * nh] + dtb_ref[...])
        r = lax.broadcasted_iota(jnp.int32, (GDN_CHUNK, GDN_CHUNK), 0)
        c = lax.broadcasted_iota(jnp.int32, (GDN_CHUNK, GDN_CHUNK), 1)
        lower = jnp.where(r >= c, 1.0, 0.0).astype(F32)
        for chunk in range(tm // GDN_CHUNK):
            rows = slice(chunk * GDN_CHUNK, (chunk + 1) * GDN_CHUNK)
            gb_ref[0, rows, :] = jnp.concatenate([_dot_f32(lower, g[rows]), beta[rows]], axis=-1)

    @pl.when(step < conv_steps)
    def _():
        is_q = step < qk_steps
        is_k = (step >= qk_steps) & (step < 2 * qk_steps)

        def project(i):
            pre = _dot(xn_ref[...], w_ref[:, i * sub:(i + 1) * sub])
            pre_ref[i, 0:CONV_HALO, :] = jnp.where(ti > 0, pre[:CONV_HALO], 0.0)
            pre_ref[i, CONV_HALO:, :] = pre[CONV_HALO:]

        def conv(i):
            ext = pre_ref[i]
            y = None
            for tap in range(GDN_CONV):
                back = GDN_CONV - 1 - tap
                rows = ext if back == 0 else pltpu.roll(ext, back, axis=0)
                term = rows[CONV_HALO:] * cw_ref[tap:tap + 1, i * sub:(i + 1) * sub]
                y = term if y is None else y + term
            y = y / (1.0 + jnp.exp2(y * (-LOG2E)))
            for head in range(sub // GDN_HEAD_DIM):
                lanes = slice(head * GDN_HEAD_DIM, (head + 1) * GDN_HEAD_DIM)
                yh = y[:, lanes]
                inv = lax.rsqrt(jnp.sum(yh * yh, axis=-1, keepdims=True) + EPS)
                mult = jnp.where(is_q, inv * (GDN_HEAD_DIM ** -0.5), jnp.where(is_k, inv, 1.0))
                qkv_ref[0, :, i * sub + head * GDN_HEAD_DIM:i * sub + (head + 1) * GDN_HEAD_DIM] = yh * mult

        _emit_skewed(cols // sub, (project, conv))

    @pl.when(step >= conv_steps)
    def _():
        z_ref[0] = _dot(xn_ref[CONV_HALO:, :], w_ref[...])


def _gdn_in(h3, gain, w, conv_w, a_log, dt_bias, tm):
    b, t, d = h3.shape
    nh = GDN_V_HEADS
    cols = GDN_IN_COLS
    conv_steps = GDN_CONV_W // cols
    steps = (GDN_CONV_W + GDN_VW) // cols
    w_ba = w[:, GDN_CONV_W + GDN_VW:]
    return pl.pallas_call(
        functools.partial(_gdn_in_body, tm=tm, cols=cols, sub=GDN_IN_SUB),
        out_shape=[
            jax.ShapeDtypeStruct((b, t, GDN_CONV_W), F32),
            jax.ShapeDtypeStruct((b, t, GDN_VW), F32),
            jax.ShapeDtypeStruct((b, t, 2 * nh), F32),
        ],
        grid=(b, t // tm, steps),
        in_specs=[
            pl.BlockSpec((1, tm, d), lambda bi, ti, s: (bi, ti, 0)),
            pl.BlockSpec((1, CONV_HALO, d),
                         lambda bi, ti, s: (bi, jnp.maximum(ti * (tm // CONV_HALO) - 1, 0), 0)),
            pl.BlockSpec((1, d), lambda bi, ti, s: (0, 0)),
            pl.BlockSpec((d, cols), lambda bi, ti, s: (0, s)),
            pl.BlockSpec(w_ba.shape, lambda bi, ti, s: (0, 0)),
            pl.BlockSpec((GDN_CONV, cols), lambda bi, ti, s: (0, jnp.minimum(s, conv_steps - 1))),
            pl.BlockSpec((1, nh), lambda bi, ti, s: (0, 0)),
            pl.BlockSpec((1, nh), lambda bi, ti, s: (0, 0)),
        ],
        out_specs=[
            pl.BlockSpec((1, tm, cols), lambda bi, ti, s: (bi, ti, jnp.minimum(s, conv_steps - 1))),
            pl.BlockSpec((1, tm, cols), lambda bi, ti, s: (bi, ti, jnp.maximum(s - conv_steps, 0))),
            pl.BlockSpec((1, tm, 2 * nh), lambda bi, ti, s: (bi, ti, 0)),
        ],
        scratch_shapes=[
            pltpu.VMEM((CONV_HALO + tm, d), BF16),
            pltpu.VMEM((cols // GDN_IN_SUB, CONV_HALO + tm, GDN_IN_SUB), F32),
        ],
        compiler_params=_params(("parallel", "parallel", "arbitrary")),
        name="gdn_in",
    )(h3, h3, gain.reshape(1, d), w, w_ba, conv_w, a_log.reshape(1, nh), dt_bias.reshape(1, nh))


def _split(x):
    hi = x.astype(BF16)
    return hi, (x - hi.astype(F32)).astype(BF16)


def _block_diag(y, mask):
    return jnp.where(mask, jnp.concatenate([y, y], axis=0), jnp.zeros((), y.dtype))


def _pair_lhs(x):
    hi, lo = _split(x)
    return jnp.concatenate([hi, hi, lo], axis=1)


def _pair_rhs(y, mask):
    hi, lo = _split(y)
    hi, lo = _block_diag(hi, mask), _block_diag(lo, mask)
    return jnp.concatenate([hi, lo, hi], axis=0)


def _pair_mm(lhs, rhs):
    return _dot(lhs, rhs)


def _unit_lower_inverse_pairs(a_list, eye, same_block, bd_mask):
    mm = lambda xs, ys: [_pair_mm(_pair_lhs(x), _pair_rhs(y, bd_mask)) for x, y in zip(xs, ys)]
    add = lambda xs, ys: [x + y for x, y in zip(xs, ys)]
    c = a_list[0].shape[0]
    ad = [jnp.where(same_block, a, 0.0) for a in a_list]
    lo = [a - d for a, d in zip(a_list, ad)]
    dinv = [eye - d for d in ad]
    power = ad
    span = 2
    while span < INV_BLOCK:
        power = mm(power, power)
        dinv = add(dinv, mm(dinv, power))
        span *= 2
    n = mm(dinv, lo)
    m = [eye - x for x in n]
    power = n
    span = 2
    while span < c // INV_BLOCK:
        power = mm(power, power)
        m = add(m, mm(m, power))
        span *= 2
    return mm(m, dinv)


def _gdn_local_body(q_ref, k_ref, v_ref, gb_ref, gr_ref, u_ref, wq_ref, kt_ref, at_ref, *, n_chunks):
    pair = pl.program_id(1)
    cs = GDN_CHUNK
    nh = GDN_V_HEADS
    d = GDN_HEAD_DIM
    rep = GDN_V_HEADS // GDN_K_HEADS
    assert rep == 2 and rep * cs == LANES

    ii = lax.broadcasted_iota(jnp.int32, (cs, rep * cs), 0)
    lane = lax.broadcasted_iota(jnp.int32, (cs, rep * cs), 1)
    jj = lane % cs
    first = lane < cs
    eye = jnp.where(ii == jj, 1.0, 0.0).astype(F32)
    same_block = (ii // INV_BLOCK) == (jj // INV_BLOCK)
    r2 = lax.broadcasted_iota(jnp.int32, (rep * cs, rep * cs), 0)
    c2 = lax.broadcasted_iota(jnp.int32, (rep * cs, rep * cs), 1)
    bd_mask = (r2 // cs) == (c2 // cs)
    lane_gb = lax.broadcasted_iota(jnp.int32, (cs, 2 * nh), 1)
    zeros_rhs = jnp.zeros((cs, 2 * d), BF16)
    chunks = range(n_chunks)
    rows = [slice(c * cs, (c + 1) * cs) for c in chunks]

    def column(gb, idx):
        return jnp.sum(jnp.where(lane_gb == idx, gb, 0.0), axis=-1, keepdims=True)

    q = [q_ref[0, r, :] for r in rows]
    k = [k_ref[0, r, :] for r in rows]
    gb = [gb_ref[0, r, :] for r in rows]
    gc = [[column(g, rep * pair + h) for h in range(rep)] for g in gb]
    beta = [[column(g, nh + rep * pair + h) for h in range(rep)] for g in gb]
    gc_row = [gr_ref[0, 0, c] for c in chunks]
    decay = [jnp.exp(jnp.where(ii >= jj, jnp.where(first, gc[c][0], gc[c][1]) - gc_row[c], -jnp.inf))
             for c in chunks]
    k16 = [x.astype(BF16) for x in k]
    kk16 = [jnp.concatenate([x, x], axis=0) for x in k16]
    a = [jnp.where(ii > jj, _dot_nt(k16[c], kk16[c]) * jnp.where(first, beta[c][0], beta[c][1]) * decay[c], 0.0)
         for c in chunks]
    for c in chunks:
        at_ref[0, 0, rows[c], :] = jnp.where(
            ii >= jj, _dot_nt(q[c].astype(BF16), kk16[c]) * decay[c], 0.0).astype(BF16)
    tinv = _unit_lower_inverse_pairs(a, eye, same_block, bd_mask)

    eg = [[jnp.exp(gc[c][h]) for h in range(rep)] for c in chunks]
    sol = []
    for c in chunks:
        rhs_hi, rhs_lo = [], []
        for h in range(rep):
            v = v_ref[0, rows[c], h * d:(h + 1) * d]
            hi, lo = _split(jnp.concatenate([v * beta[c][h], k[c] * (beta[c][h] * eg[c][h])], axis=-1))
            pad = [zeros_rhs] * rep
            rhs_hi.append(jnp.concatenate(pad[:h] + [hi] + pad[h + 1:], axis=1))
            rhs_lo.append(jnp.concatenate(pad[:h] + [lo] + pad[h + 1:], axis=1))
        rhs_hi = jnp.concatenate(rhs_hi, axis=0)
        rhs_lo = jnp.concatenate(rhs_lo, axis=0)
        sol.append(_dot(_pair_lhs(tinv[c]), jnp.concatenate([rhs_hi, rhs_lo, rhs_hi], axis=0)))

    for c in chunks:
        for h in range(rep):
            u_ref[0, rows[c], h * d:(h + 1) * d] = sol[c][:, 2 * d * h:2 * d * h + d]
            wq_ref[0, h, c, :cs, :] = sol[c][:, 2 * d * h + d:2 * d * (h + 1)].astype(BF16)
            wq_ref[0, h, c, cs:, :] = (q[c] * eg[c][h]).astype(BF16)
            g_last = gc_row[c][:, h * cs + cs - 1:(h + 1) * cs]
            kt_ref[0, h, c] = (k[c] * jnp.exp(g_last - gc[c][h])).T.astype(BF16)


def _gdn_local(qkv, gb, gc_rows, n_chunks):
    b, t, _ = qkv.shape
    nh = GDN_V_HEADS
    rep = GDN_V_HEADS // GDN_K_HEADS
    cs = GDN_CHUNK
    tt = n_chunks * cs
    d = GDN_HEAD_DIM
    return pl.pallas_call(
        functools.partial(_gdn_local_body, n_chunks=n_chunks),
        out_shape=[
            jax.ShapeDtypeStruct((b, t, GDN_VW), F32),
            jax.ShapeDtypeStruct((b, nh, t // cs, 2 * cs, d), BF16),
            jax.ShapeDtypeStruct((b, nh, t // cs, d, cs), BF16),
            jax.ShapeDtypeStruct((b, GDN_K_HEADS, t, rep * cs), BF16),
        ],
        grid=(b, GDN_K_HEADS, t // tt),
        in_specs=[
            pl.BlockSpec((1, tt, d), lambda bi, p, ti: (bi, ti, p)),
            pl.BlockSpec((1, tt, d), lambda bi, p, ti: (bi, ti, GDN_K_HEADS + p)),
            pl.BlockSpec((1, tt, rep * d), lambda bi, p, ti: (bi, ti, GDN_K_HEADS + p)),
            pl.BlockSpec((1, tt, 2 * nh), lambda bi, p, ti: (bi, ti, 0)),
            pl.BlockSpec((1, 1, n_chunks, 1, rep * cs), lambda bi, p, ti: (bi, p, ti, 0, 0)),
        ],
        out_specs=[
            pl.BlockSpec((1, tt, rep * d), lambda bi, p, ti: (bi, ti, p)),
            pl.BlockSpec((1, rep, n_chunks, 2 * cs, d), lambda bi, p, ti: (bi, p, ti, 0, 0)),
            pl.BlockSpec((1, rep, n_chunks, d, cs), lambda bi, p, ti: (bi, p, ti, 0, 0)),
            pl.BlockSpec((1, 1, tt, rep * cs), lambda bi, p, ti: (bi, p, ti, 0)),
        ],
        compiler_params=_params(("parallel", "parallel", "parallel")),
        name="gdn_local",
    )(qkv, qkv, qkv, gb, gc_rows)


def _gdn_scan_body(u_ref, wq_ref, kt_ref, at_ref, gr_ref, z_ref, g_ref, y_ref, s_ref, *, n_chunks, n_pairs):
    cs = GDN_CHUNK
    d = GDN_HEAD_DIM
    rep = GDN_V_HEADS // GDN_K_HEADS
    heads = range(rep * n_pairs)

    @pl.when(pl.program_id(2) == 0)
    def _():
        s_ref[...] = jnp.zeros_like(s_ref)

    zeros_v = jnp.zeros((cs, d), BF16)
    states = [s_ref[i] for i in heads]
    for c in range(n_chunks):
        rows = slice(c * cs, (c + 1) * cs)
        s16 = [s.astype(BF16) for s in states]
        both = [_dot(wq_ref[0, i, c], s16[i]) for i in heads]
        v_new = [(u_ref[0, rows, i * d:(i + 1) * d] - both[i][:cs]).astype(BF16) for i in heads]
        for p in range(n_pairs):
            gc_row = gr_ref[0, p, c]
            diag = []
            for h in range(rep):
                pad = [zeros_v] * rep
                diag.append(jnp.concatenate(pad[:h] + [v_new[rep * p + h]] + pad[h + 1:], axis=1))
            o_attn = _dot(at_ref[0, p, rows, :], jnp.concatenate(diag, axis=0))
            for h in range(rep):
                i = rep * p + h
                o = both[i][cs:] + o_attn[:, h * d:(h + 1) * d]
                z = z_ref[0, rows, i * d:(i + 1) * d]
                gate = z / (1.0 + jnp.exp2(z * (-LOG2E)))
                y_ref[0, rows, i * d:(i + 1) * d] = (o * _rms_scale(o) * g_ref[...] * gate).astype(BF16)
                g_last = gc_row[:, h * cs + cs - 1:(h + 1) * cs]
                states[i] = states[i] * jnp.exp(g_last) + _dot(kt_ref[0, i, c], v_new[i])
    for i in heads:
        s_ref[i] = states[i]


def _gdn_scan(u, wq, kt, attn, gc_rows, z, gain, n_chunks, n_pairs):
    b, t, _ = u.shape
    rep = GDN_V_HEADS // GDN_K_HEADS
    cs = GDN_CHUNK
    tt = n_chunks * cs
    d = GDN_HEAD_DIM
    nhs = rep * n_pairs
    return pl.pallas_call(
        functools.partial(_gdn_scan_body, n_chunks=n_chunks, n_pairs=n_pairs),
        out_shape=jax.ShapeDtypeStruct((b, t, GDN_VW), BF16),
        grid=(b, GDN_K_HEADS // n_pairs, t // tt),
        in_specs=[
            pl.BlockSpec((1, tt, nhs * d), lambda bi, g, ti: (bi, ti, g)),
            pl.BlockSpec((1, nhs, n_chunks, 2 * cs, d), lambda bi, g, ti: (bi, g, ti, 0, 0)),
            pl.BlockSpec((1, nhs, n_chunks, d, cs), lambda bi, g, ti: (bi, g, ti, 0, 0)),
            pl.BlockSpec((1, n_pairs, tt, rep * cs), lambda bi, g, ti: (bi, g, ti, 0)),
            pl.BlockSpec((1, n_pairs, n_chunks, 1, rep * cs), lambda bi, g, ti: (bi, g, ti, 0, 0)),
            pl.BlockSpec((1, tt, nhs * d), lambda bi, g, ti: (bi, ti, g)),
            pl.BlockSpec((1, d), lambda bi, g, ti: (0, 0)),
        ],
        out_specs=pl.BlockSpec((1, tt, nhs * d), lambda bi, g, ti: (bi, ti, g)),
        scratch_shapes=[pltpu.VMEM((nhs, d, d), F32)],
        compiler_params=_params(("parallel", "parallel", "arbitrary")),
        name="gdn_scan",
    )(u, wq, kt, attn, gc_rows, z, gain.reshape(1, d))


def _attention_mixer(h, b, t, gain, w_in, q_gain, k_gain, sinks, w_out, tm):
    d = h.shape[1]
    swa_w = SWA_QW + 2 * SWA_KVW
    outs = ((SB_W, BF16, HEAD_DIM ** -0.5 * LOG2E), (2 * SB_W, BF16, None), (swa_w, F32, None))
    sb_q, sb_kv, swa = _norm_proj(h, gain, w_in.astype(BF16), outs, tm)
    a_out = _sb_attention(sb_q.reshape(b, t, SB_W), sb_kv.reshape(b, t, 2 * SB_W),
                          min(SB_BLOCK, t), SB_Q_BLOCKS if t >= SB_Q_BLOCKS * SB_BLOCK else 1,
                          SB_PAIRS).reshape(b * t, SB_W)
    b_out = _swa_attention(swa.reshape(b, t, swa_w), q_gain, k_gain, sinks,
                           min(SWA_TILE_BLOCKS, t // WINDOW)).reshape(b * t, SWA_QW)
    return [a_out, b_out], w_out.astype(BF16)


def _gdn_mixer(h, b, t, gain, w_in, conv_w, a_log, dt_bias, out_gain, w_out, tm):
    nh = GDN_V_HEADS
    ba_w = w_in.shape[1] - GDN_CONV_W - GDN_VW
    assert ba_w == 2 * nh
    qkv, z, gb = _gdn_in(h.reshape(b, t, -1), gain, w_in.astype(BF16), conv_w, a_log, dt_bias,
                         min(GDN_IN_TILE, t))
    rep = GDN_V_HEADS // GDN_K_HEADS
    gc_rows = gb[:, :, :nh].reshape(b, t // GDN_CHUNK, GDN_CHUNK, GDN_K_HEADS, rep)
    gc_rows = jnp.transpose(gc_rows, (0, 3, 1, 4, 2)).reshape(
        b, GDN_K_HEADS, t // GDN_CHUNK, 1, rep * GDN_CHUNK)
    u, wq, kt, attn = _gdn_local(qkv, gb, gc_rows, min(GDN_LOCAL_CHUNKS, t // GDN_CHUNK))
    y = _gdn_scan(u, wq, kt, attn, gc_rows, z, out_gain, min(GDN_SCAN_CHUNKS, t // GDN_CHUNK), GDN_SCAN_PAIRS)
    return [y.reshape(b * t, GDN_VW)], w_out.astype(BF16)


def kernel(x, p, ffn_norm, ffn_w_gate, ffn_w_up, ffn_w_down, mix_norm, att_w_in, att_q_norm, att_k_norm, att_sinks, att_w_out, gdn_w_in, gdn_conv_w, gdn_a_log, gdn_dt_bias, gdn_out_norm, gdn_w_out, ple_norm, ple_w_gate, ple_w_proj):
    b, t, d = x.shape
    n = b * t
    tm = min(TOKEN_TILE, n)
    depth = p.shape[0]
    h = x.reshape(n, d)
    wg, wu, wd = ffn_w_gate.astype(BF16), ffn_w_up.astype(BF16), ffn_w_down.astype(BF16)
    for i in range(depth):
        j = i // 2
        h = _ffn(h, ffn_norm[i, 0], wg, wu, wd, i, 0, tm)
        if i % 2 == 0:
            mix = _attention_mixer(h, b, t, mix_norm[i], att_w_in[j], att_q_norm[j], att_k_norm[j],
                                   att_sinks[j], att_w_out[j], tm)
        else:
            mix = _gdn_mixer(h, b, t, mix_norm[i], gdn_w_in[j], gdn_conv_w[j], gdn_a_log[j],
                             gdn_dt_bias[j], gdn_out_norm[j], gdn_w_out[j], tm)
        ple = (p.reshape(depth, n, -1), ple_norm[i], ple_w_gate[i].astype(BF16), ple_w_proj[i].astype(BF16))
        h = _ffn(h, ffn_norm[i, 1], wg, wu, wd, i, 1, tm, mix=mix, ple=ple)
    return h.reshape(b, t, d)
```

```python
import functools

import jax
import jax.numpy as jnp
from jax import lax
from jax.experimental import pallas as pl
from jax.experimental.pallas import tpu as pltpu

F32 = jnp.float32
BF16 = jnp.bfloat16

EPS = 1e-6
LOG2E = 1.4426950408889634
HEAD_DIM = 64
SB_HEADS = 8
SWA_HEADS = 8
SWA_KV_HEADS = 2
WINDOW = 128
GDN_K_HEADS = 8
GDN_V_HEADS = 16
GDN_HEAD_DIM = 128
GDN_CONV = 4
GDN_CHUNK = 64
SB_W = SB_HEADS * HEAD_DIM
SWA_QW = SWA_HEADS * HEAD_DIM
SWA_KVW = SWA_KV_HEADS * HEAD_DIM
GDN_KW = GDN_K_HEADS * GDN_HEAD_DIM
GDN_VW = GDN_V_HEADS * GDN_HEAD_DIM
GDN_CONV_W = 2 * GDN_KW + GDN_VW

LANES = 128
VMEM_LIMIT = 56 * 1024 * 1024

TOKEN_TILE = 512
FF_CHUNK = 256
SB_BLOCK = 256
SB_Q_BLOCKS = 2
SB_CUMSUM_TERMS = 1
SB_PAIRS = 4
SWA_TILE_BLOCKS = 4
GDN_IN_TILE = 1024
GDN_IN_COLS = 1024
GDN_IN_SUB = 256
CONV_HALO = 16
GDN_LOCAL_CHUNKS = 16
GDN_SCAN_CHUNKS = 8
GDN_SCAN_PAIRS = 4
INV_BLOCK = 16


def _params(sem):
    return pltpu.CompilerParams(dimension_semantics=sem, vmem_limit_bytes=VMEM_LIMIT)


def _dot(a, b):
    return jnp.dot(a, b, preferred_element_type=F32)


def _dot_nt(a, b):
    return lax.dot_general(a, b, (((1,), (1,)), ((), ())), preferred_element_type=F32)


def _dot_f32(a, b):
    return jnp.dot(a, b, preferred_element_type=F32, precision=lax.Precision.HIGHEST)


def _rms_scale(x):
    return lax.rsqrt(jnp.mean(x * x, axis=-1, keepdims=True) + EPS)


def _sigmoid(x):
    return 1.0 / (1.0 + jnp.exp(-x))


def _emit_skewed(n, stages):
    for tick in range(n + len(stages) - 1):
        for s, stage in enumerate(stages):
            if 0 <= tick - s < n:
                stage(tick - s)


def _neg_abs(x):
    bits = lax.bitcast_convert_type(x, jnp.uint32) | jnp.uint32(0x80000000)
    return lax.bitcast_convert_type(bits, F32)


def _softplus(x):
    return jnp.maximum(x, 0.0) + jnp.log1p(jnp.exp(-jnp.abs(x)))


def _ffn_body(*refs, n_chunks, fc, n_mix, ple):
    refs = list(refs)
    x_ref, g_ref, wg_ref, wu_ref, wd_ref = refs[:5]
    refs = refs[5:]
    mix_refs, refs = refs[:n_mix], refs[n_mix:]
    if n_mix:
        wmix_ref, refs = refs[0], refs[1:]
    if ple:
        p_ref, pg_ref, pwg_ref, pwp_ref = refs[:4]
        refs = refs[4:]
    o_ref, acc_ref = refs

    x = x_ref[...]
    start = 0
    for m_ref in mix_refs:
        width = m_ref.shape[1]
        x = x + _dot(m_ref[...].astype(BF16), wmix_ref[start:start + width, :])
        start += width
    xn = (x * _rms_scale(x) * g_ref[...]).astype(BF16)
    for c in range(n_chunks):
        cols = slice(c * fc, (c + 1) * fc)
        gate = _dot(xn, wg_ref[:, cols])
        up = _dot(xn, wu_ref[:, cols])
        act = (gate * _sigmoid(gate) * up).astype(BF16)
        part = _dot(act, wd_ref[cols, :])
        if c == 0:
            acc_ref[...] = part
        else:
            acc_ref[...] += part
    y = x + 0.5 * acc_ref[...]
    if ple:
        yn = (y * _rms_scale(y) * pg_ref[...]).astype(BF16)
        y = y + _sigmoid(_dot(yn, pwg_ref[...])) * _dot(p_ref[0].astype(BF16), pwp_ref[...])
    o_ref[...] = y


def _ffn(h, gain, wg, wu, wd, layer, half, tm, mix=None, ple=None):
    n, d = h.shape
    f = wg.shape[-1]
    fc = FF_CHUNK if f % FF_CHUNK == 0 else f
    const = lambda i: (0, 0)
    row = lambda i: (i, 0)
    pick = lambda i: (layer, half, 0, 0)
    whole = lambda a: pl.BlockSpec(a.shape, const, pipeline_mode=pl.Buffered(1))
    args = [h, gain.reshape(1, d), wg, wu, wd]
    in_specs = [
        pl.BlockSpec((tm, d), row),
        pl.BlockSpec((1, d), const),
        pl.BlockSpec((None, None, d, f), pick, pipeline_mode=pl.Buffered(1)),
        pl.BlockSpec((None, None, d, f), pick, pipeline_mode=pl.Buffered(1)),
        pl.BlockSpec((None, None, f, d), pick, pipeline_mode=pl.Buffered(1)),
    ]
    n_mix = 0
    if mix is not None:
        xs, w_mix = mix
        assert sum(x.shape[1] for x in xs) == w_mix.shape[0]
        n_mix = len(xs)
        args += list(xs) + [w_mix]
        in_specs += [pl.BlockSpec((tm, x.shape[1]), row) for x in xs] + [whole(w_mix)]
    if ple is not None:
        p, p_gain, w_gate, w_proj = ple
        args += [p, p_gain.reshape(1, d), w_gate, w_proj]
        in_specs += [pl.BlockSpec((1, tm, p.shape[2]), lambda i: (layer, i, 0)),
                     pl.BlockSpec((1, d), const), whole(w_gate), whole(w_proj)]
    return pl.pallas_call(
        functools.partial(_ffn_body, n_chunks=f // fc, fc=fc, n_mix=n_mix, ple=ple is not None),
        out_shape=jax.ShapeDtypeStruct((n, d), F32),
        grid=(n // tm,),
        in_specs=in_specs,
        out_specs=pl.BlockSpec((tm, d), row),
        scratch_shapes=[pltpu.VMEM((tm, d), F32)],
        compiler_params=_params(("parallel",)),
        name="ffn",
    )(*args)


def _norm_proj_body(x_ref, g_ref, w_ref, *o_refs, outs):
    x = x_ref[...]
    xn = (x * _rms_scale(x) * g_ref[...]).astype(BF16)
    start = 0
    for o_ref, (width, _, scale) in zip(o_refs, outs):
        y = _dot(xn, w_ref[:, start:start + width])
        o_ref[...] = (y if scale is None else y * scale).astype(o_ref.dtype)
        start += width


def _norm_proj(h, gain, w, outs, tm):
    n, d = h.shape
    widths = [o[0] for o in outs]
    assert sum(widths) == w.shape[1]
    return pl.pallas_call(
        functools.partial(_norm_proj_body, outs=outs),
        out_shape=[jax.ShapeDtypeStruct((n, width), dtype) for width, dtype, _ in outs],
        grid=(n // tm,),
        in_specs=[
            pl.BlockSpec((tm, d), lambda i: (i, 0)),
            pl.BlockSpec((1, d), lambda i: (0, 0)),
            pl.BlockSpec(w.shape, lambda i: (0, 0), pipeline_mode=pl.Buffered(1)),
        ],
        out_specs=[pl.BlockSpec((tm, width), lambda i: (i, 0)) for width in widths],
        compiler_params=_params(("parallel",)),
        name="norm_proj",
    )(h, gain.reshape(1, d), w)


def _sb_body(q_ref, k_ref, v_ref, o_ref, acc_ref, *, blk, qsub, n_pairs):
    qi = pl.program_id(2)
    per = LANES // HEAD_DIM
    assert per == 2
    tq = qsub * blk
    later = jnp.where(lax.broadcasted_iota(jnp.int32, (blk, blk), 0)
                      >= lax.broadcasted_iota(jnp.int32, (blk, blk), 1), 1.0, 0.0).astype(BF16)
    later2 = jnp.concatenate([later, later], axis=0)
    first = lax.broadcasted_iota(jnp.int32, (tq, LANES), 1) < HEAD_DIM
    heads = [(p, h) for p in range(n_pairs) for h in range(per)]
    q = []
    for p, h in heads:
        qp = q_ref[0, :, p * LANES:(p + 1) * LANES]
        q.append(jnp.where(first if h == 0 else ~first, qp, jnp.zeros_like(qp)))

    def block(j, carry, diag):
        start = pl.multiple_of(j * blk, blk)
        rows = pl.ds(start, blk)
        r0 = 0 if diag is None else diag * blk
        causal = None
        if diag is not None:
            causal = (lax.broadcasted_iota(jnp.int32, (tq - r0, blk), 1)
                      < lax.broadcasted_iota(jnp.int32, (tq - r0, blk), 0))
        n = len(heads)
        z, split, rc, w, pv = [None] * n, [None] * n, [None] * n, [None] * n, [None] * n

        def scores(i):
            p = heads[i][0]
            z[i] = _dot_nt(q[i][r0:], k_ref[0, rows, p * LANES:(p + 1) * LANES])

        def keep_terms(i):
            sp = jnp.maximum(z[i], 0.0) + jnp.log(1.0 + jnp.exp2(_neg_abs(z[i]))) * LOG2E
            if causal is not None:
                sp = jnp.where(causal, sp, 0.0)
            hi = sp.astype(BF16)
            if SB_CUMSUM_TERMS == 1:
                split[i] = hi
            else:
                split[i] = jnp.concatenate([hi, (sp - hi.astype(F32)).astype(BF16)], axis=1)

        def cumulate(i):
            rc[i] = _dot(split[i], later if SB_CUMSUM_TERMS == 1 else later2)

        def weights(i):
            x = z[i] - rc[i]
            if carry is not None:
                x = x - carry[i]
            x = jnp.exp2(x)
            if causal is not None:
                x = jnp.where(causal, x, 0.0)
            w[i] = x.astype(BF16)

        def values(i):
            p = heads[i][0]
            pv[i] = _dot(w[i], v_ref[0, rows, p * LANES:(p + 1) * LANES])

        _emit_skewed(n, (scores, keep_terms, cumulate, weights, values))
        return pv, [r[:, 0:1] for r in rc]

    carry = None
    for diag in reversed(range(qsub)):
        r0 = diag * blk
        pv, tot = block(qi * qsub + diag, None if carry is None else [c[r0:] for c in carry], diag)
        if r0:
            tot = [jnp.concatenate([jnp.zeros((r0, 1), F32), t], axis=0) for t in tot]
        for i, x in enumerate(pv):
            if carry is None:
                if r0:
                    acc_ref[i, :r0, :] = jnp.zeros((r0, LANES), F32)
                acc_ref[i, r0:, :] = x
            else:
                acc_ref[i, r0:, :] += x
        carry = tot if carry is None else [c + t for c, t in zip(carry, tot)]

    def step(it, carry):
        pv, tot = block(qi * qsub - 1 - it, carry, None)
        for i, x in enumerate(pv):
            acc_ref[i] += x
        return tuple(c + t for c, t in zip(carry, tot))

    lax.fori_loop(0, qi * qsub, step, tuple(carry))
    for p in range(n_pairs):
        o_ref[0, :, p * LANES:(p + 1) * LANES] = jnp.where(first, acc_ref[per * p], acc_ref[per * p + 1])


def _sb_attention(q, kv, blk, qsub, n_pairs):
    b, t, _ = q.shape
    width = n_pairs * LANES
    groups = SB_W // width
    tq = qsub * blk
    return pl.pallas_call(
        functools.partial(_sb_body, blk=blk, qsub=qsub, n_pairs=n_pairs),
        out_shape=jax.ShapeDtypeStruct((b, t, SB_W), F32),
        grid=(b, groups, t // tq),
        in_specs=[
            pl.BlockSpec((1, tq, width), lambda bi, g, qi: (bi, qi, g)),
            pl.BlockSpec((1, t, width), lambda bi, g, qi: (bi, 0, g)),
            pl.BlockSpec((1, t, width), lambda bi, g, qi: (bi, 0, groups + g)),
        ],
        out_specs=pl.BlockSpec((1, tq, width), lambda bi, g, qi: (bi, qi, g)),
        scratch_shapes=[pltpu.VMEM((2 * n_pairs, tq, LANES), F32)],
        compiler_params=_params(("parallel", "parallel", "arbitrary")),
        name="sb_attention",
    )(q, kv, kv)


def _half_rms_scale(x, first):
    sq = x * x
    ms0 = jnp.sum(jnp.where(first, sq, 0.0), axis=-1, keepdims=True) * (1.0 / HEAD_DIM)
    ms1 = jnp.sum(jnp.where(first, 0.0, sq), axis=-1, keepdims=True) * (1.0 / HEAD_DIM)
    return jnp.where(first, lax.rsqrt(ms0 + EPS), lax.rsqrt(ms1 + EPS))


def _swa_body(sink_ref, q_ref, kvc_ref, kvp_ref, qg_ref, kg_ref, o_ref, *, n_blocks, slopes):
    tile = pl.program_id(1)
    w = WINDOW
    group = SWA_HEADS // SWA_KV_HEADS
    per = LANES // HEAD_DIM
    assert per == 2 and SWA_KVW == LANES
    first = lax.broadcasted_iota(jnp.int32, (1, LANES), 1) < HEAD_DIM

    def both_halves(x):
        swapped = pltpu.roll(x, HEAD_DIM, axis=1)
        return jnp.where(first, x, swapped), jnp.where(first, swapped, x)

    kv = jnp.concatenate([kvp_ref[0], kvc_ref[0]], axis=0)
    k = kv[:, :SWA_KVW]
    k = k * _half_rms_scale(k, first) * kg_ref[...]
    k_dup = [x.astype(BF16) for x in both_halves(k)]
    v_dup = [x.astype(BF16) for x in both_halves(kv[:, SWA_KVW:])]

    q_heads = []
    for g2 in range(SWA_QW // LANES):
        q = q_ref[0, :, g2 * LANES:(g2 + 1) * LANES]
        q = q * _half_rms_scale(q, first) * (qg_ref[...] * (HEAD_DIM ** -0.5 * LOG2E))
        q_heads.append(jnp.where(first, q, 0.0).astype(BF16))
        q_heads.append(jnp.where(first, 0.0, q).astype(BF16))

    qpos = lax.broadcasted_iota(jnp.int32, (w, 2 * w), 0)
    kpos = lax.broadcasted_iota(jnp.int32, (w, 2 * w), 1)
    dist = qpos + w - kpos
    band = (dist >= 0) & (dist < w)
    distf = dist.astype(F32)
    has_prev = (tile > 0) | (kpos >= w)
    bias, sink = [], []
    for kvh in range(SWA_KV_HEADS):
        heads = range(kvh * group, (kvh + 1) * group)
        bias.append(jnp.concatenate(
            [jnp.where(band, (-slopes[h] * LOG2E) * distf, -jnp.inf) for h in heads], axis=0))
        sink.append(jnp.concatenate(
            [jnp.full((w, 1), sink_ref[h] * LOG2E, F32) for h in heads], axis=0))

    units = [(j, kvh) for j in range(n_blocks) for kvh in range(SWA_KV_HEADS)]
    n = len(units)
    s, e, denom, res = [None] * n, [None] * n, [None] * n, [None] * n

    def scores(i):
        j, kvh = units[i]
        q4 = jnp.concatenate([q_heads[kvh * group + g][j * w:(j + 1) * w] for g in range(group)], axis=0)
        s[i] = _dot_nt(q4, k_dup[kvh][j * w:(j + 2) * w]) + bias[kvh]
        if j == 0:
            s[i] = jnp.where(jnp.concatenate([has_prev] * group, axis=0), s[i], -jnp.inf)

    def softmax(i):
        kvh = units[i][1]
        m = jnp.maximum(jnp.max(s[i], axis=-1, keepdims=True), sink[kvh])
        p = jnp.exp2(s[i] - m)
        denom[i] = jnp.sum(p, axis=-1, keepdims=True) + jnp.exp2(sink[kvh] - m)
        e[i] = p.astype(BF16)

    def values(i):
        j, kvh = units[i]
        res[i] = _dot(e[i], v_dup[kvh][j * w:(j + 2) * w]) / denom[i]

    def store(i):
        j, kvh = units[i]
        for g2 in range(group // per):
            piece = [res[i][(per * g2 + h) * w:(per * g2 + h + 1) * w] for h in range(per)]
            lanes = (kvh * (group // per) + g2) * LANES
            o_ref[0, j * w:(j + 1) * w, lanes:lanes + LANES] = jnp.where(first, piece[0], piece[1])

    _emit_skewed(n, (scores, softmax, values, store))


def _swa_attention(proj, q_gain, k_gain, sinks, n_blocks):
    b, t, _ = proj.shape
    w = WINDOW
    tq = n_blocks * w
    kv_blk = SWA_QW // (2 * SWA_KVW)
    slopes = tuple(float(2.0 ** (-8.0 * (i + 1) / SWA_HEADS)) for i in range(SWA_HEADS))
    gain2 = lambda g: jnp.concatenate([g, g]).reshape(1, LANES)
    return pl.pallas_call(
        functools.partial(_swa_body, n_blocks=n_blocks, slopes=slopes),
        out_shape=jax.ShapeDtypeStruct((b, t, SWA_QW), F32),
        grid=(b, t // tq),
        in_specs=[
            pl.BlockSpec(memory_space=pltpu.SMEM),
            pl.BlockSpec((1, tq, SWA_QW), lambda bi, i: (bi, i, 0)),
            pl.BlockSpec((1, tq, 2 * SWA_KVW), lambda bi, i: (bi, i, kv_blk)),
            pl.BlockSpec((1, w, 2 * SWA_KVW), lambda bi, i: (bi, jnp.maximum(i * n_blocks - 1, 0), kv_blk)),
            pl.BlockSpec((1, LANES), lambda bi, i: (0, 0)),
            pl.BlockSpec((1, LANES), lambda bi, i: (0, 0)),
        ],
        out_specs=pl.BlockSpec((1, tq, SWA_QW), lambda bi, i: (bi, i, 0)),
        compiler_params=_params(("parallel", "parallel")),
        name="swa_attention",
    )(sinks, proj, proj, proj, gain2(q_gain), gain2(k_gain))


def _gdn_in_body(x_ref, halo_ref, g_ref, w_ref, wba_ref, cw_ref, alog_ref, dtb_ref,
                 qkv_ref, z_ref, gb_ref, xn_ref, pre_ref, *, tm, cols, sub):
    ti = pl.program_id(1)
    step = pl.program_id(2)
    conv_steps = GDN_CONV_W // cols
    qk_steps = GDN_KW // cols
    nh = GDN_V_HEADS

    @pl.when(step == 0)
    def _():
        xh = halo_ref[0]
        x = x_ref[0]
        xn_ref[0:CONV_HALO, :] = (xh * _rms_scale(xh) * g_ref[...]).astype(BF16)
        xn_ref[CONV_HALO:, :] = (x * _rms_scale(x) * g_ref[...]).astype(BF16)
        ba = _dot(xn_ref[CONV_HALO:, :], wba_ref[...])
        beta = _sigmoid(ba[:, :nh])
        g = -jnp.exp(alog_ref[...]) * _softplus(ba[:, nh:2 * nh] + dtb_ref[...])
        r = lax.broadcasted_iota(jnp.int32, (GDN_CHUNK, GDN_CHUNK), 0)
        c = lax.broadcasted_iota(jnp.int32, (GDN_CHUNK, GDN_CHUNK), 1)
        lower = jnp.where(r >= c, 1.0, 0.0).astype(F32)
        for chunk in range(tm // GDN_CHUNK):
            rows = slice(chunk * GDN_CHUNK, (chunk + 1) * GDN_CHUNK)
            gb_ref[0, rows, :] = jnp.concatenate([_dot_f32(lower, g[rows]), beta[rows]], axis=-1)

    def conv_step(normalise):
        def project(i):
            pre = _dot(xn_ref[...], w_ref[:, i * sub:(i + 1) * sub])
            pre_ref[i, 0:CONV_HALO, :] = jnp.where(ti > 0, pre[:CONV_HALO], 0.0)
            pre_ref[i, CONV_HALO:, :] = pre[CONV_HALO:]

        def conv(i):
            ext = pre_ref[i]
            y = None
            for tap in range(GDN_CONV):
                back = GDN_CONV - 1 - tap
                rows = ext if back == 0 else pltpu.roll(ext, back, axis=0)
                term = rows[CONV_HALO:] * cw_ref[tap:tap + 1, i * sub:(i + 1) * sub]
                y = term if y is None else y + term
            y = y / (1.0 + jnp.exp2(y * (-LOG2E)))
            if not normalise:
                qkv_ref[0, :, i * sub:(i + 1) * sub] = y
                return
            q_scale = jnp.where(step < qk_steps, GDN_HEAD_DIM ** -0.5, 1.0)
            for head in range(sub // GDN_HEAD_DIM):
                lanes = slice(head * GDN_HEAD_DIM, (head + 1) * GDN_HEAD_DIM)
                yh = y[:, lanes]
                inv = lax.rsqrt(jnp.sum(yh * yh, axis=-1, keepdims=True) + EPS) * q_scale
                qkv_ref[0, :, i * sub + head * GDN_HEAD_DIM:i * sub + (head + 1) * GDN_HEAD_DIM] = yh * inv

        _emit_skewed(cols // sub, (project, conv))

    pl.when(step < 2 * qk_steps)(functools.partial(conv_step, True))
    pl.when((step >= 2 * qk_steps) & (step < conv_steps))(functools.partial(conv_step, False))

    @pl.when(step >= conv_steps)
    def _():
        z_ref[0] = _dot(xn_ref[CONV_HALO:, :], w_ref[...])


def _gdn_in(h3, gain, w, conv_w, a_log, dt_bias, tm):
    b, t, d = h3.shape
    nh = GDN_V_HEADS
    cols = GDN_IN_COLS
    conv_steps = GDN_CONV_W // cols
    steps = (GDN_CONV_W + GDN_VW) // cols
    w_ba = w[:, GDN_CONV_W + GDN_VW:]
    return pl.pallas_call(
        functools.partial(_gdn_in_body, tm=tm, cols=cols, sub=GDN_IN_SUB),
        out_shape=[
            jax.ShapeDtypeStruct((b, t, GDN_CONV_W), F32),
            jax.ShapeDtypeStruct((b, t, GDN_VW), F32),
            jax.ShapeDtypeStruct((b, t, 2 * nh), F32),
        ],
        grid=(b, t // tm, steps),
        in_specs=[
            pl.BlockSpec((1, tm, d), lambda bi, ti, s: (bi, ti, 0)),
            pl.BlockSpec((1, CONV_HALO, d),
                         lambda bi, ti, s: (bi, jnp.maximum(ti * (tm // CONV_HALO) - 1, 0), 0)),
            pl.BlockSpec((1, d), lambda bi, ti, s: (0, 0)),
            pl.BlockSpec((d, cols), lambda bi, ti, s: (0, s)),
            pl.BlockSpec(w_ba.shape, lambda bi, ti, s: (0, 0)),
            pl.BlockSpec((GDN_CONV, cols), lambda bi, ti, s: (0, jnp.minimum(s, conv_steps - 1))),
            pl.BlockSpec((1, nh), lambda bi, ti, s: (0, 0)),
            pl.BlockSpec((1, nh), lambda bi, ti, s: (0, 0)),
        ],
        out_specs=[
            pl.BlockSpec((1, tm, cols), lambda bi, ti, s: (bi, ti, jnp.minimum(s, conv_steps - 1))),
            pl.BlockSpec((1, tm, cols), lambda bi, ti, s: (bi, ti, jnp.maximum(s - conv_steps, 0))),
            pl.BlockSpec((1, tm, 2 * nh), lambda bi, ti, s: (bi, ti, 0)),
        ],
        scratch_shapes=[
            pltpu.VMEM((CONV_HALO + tm, d), BF16),
            pltpu.VMEM((cols // GDN_IN_SUB, CONV_HALO + tm, GDN_IN_SUB), F32),
        ],
        compiler_params=_params(("parallel", "parallel", "arbitrary")),
        name="gdn_in",
    )(h3, h3, gain.reshape(1, d), w, w_ba, conv_w, a_log.reshape(1, nh), dt_bias.reshape(1, nh))


def _split(x):
    hi = x.astype(BF16)
    return hi, (x - hi.astype(F32)).astype(BF16)


def _block_diag(y, mask):
    return jnp.where(mask, jnp.concatenate([y, y], axis=0), jnp.zeros((), y.dtype))


def _mm3(x, y_hi, y_lo):
    x_hi, x_lo = _split(x)
    return _dot(jnp.concatenate([x_hi, x_hi, x_lo], axis=1), jnp.concatenate([y_hi, y_lo, y_hi], axis=0))


def _pair_mm(x, y, mask):
    y_hi, y_lo = _split(y)
    return _mm3(x, _block_diag(y_hi, mask), _block_diag(y_lo, mask))


def _unit_lower_inverse_pairs(a_list, eye, same_block, bd_mask):
    mm = lambda xs, ys: [_pair_mm(x, y, bd_mask) for x, y in zip(xs, ys)]
    add = lambda xs, ys: [x + y for x, y in zip(xs, ys)]
    c = a_list[0].shape[0]
    ad = [jnp.where(same_block, a, 0.0) for a in a_list]
    lo = [a - d for a, d in zip(a_list, ad)]
    dinv = [eye - d for d in ad]
    power = ad
    span = 2
    while span < INV_BLOCK:
        power = mm(power, power)
        dinv = add(dinv, mm(dinv, power))
        span *= 2
    n = mm(dinv, lo)
    m = [eye - x for x in n]
    power = n
    span = 2
    while span < c // INV_BLOCK:
        power = mm(power, power)
        m = add(m, mm(m, power))
        span *= 2
    return mm(m, dinv)


def _gdn_local_body(q_ref, k_ref, v_ref, gb_ref, gr_ref, u_ref, wq_ref, kt_ref, at_ref, *, n_chunks):
    pair = pl.program_id(1)
    cs = GDN_CHUNK
    nh = GDN_V_HEADS
    d = GDN_HEAD_DIM
    rep = GDN_V_HEADS // GDN_K_HEADS
    assert rep == 2 and rep * cs == LANES

    ii = lax.broadcasted_iota(jnp.int32, (cs, rep * cs), 0)
    lane = lax.broadcasted_iota(jnp.int32, (cs, rep * cs), 1)
    jj = lane % cs
    first = lane < cs
    eye = jnp.where(ii == jj, 1.0, 0.0).astype(F32)
    same_block = (ii // INV_BLOCK) == (jj // INV_BLOCK)
    r2 = lax.broadcasted_iota(jnp.int32, (rep * cs, rep * cs), 0)
    c2 = lax.broadcasted_iota(jnp.int32, (rep * cs, rep * cs), 1)
    bd_mask = (r2 // cs) == (c2 // cs)
    lane_gb = lax.broadcasted_iota(jnp.int32, (cs, 2 * nh), 1)
    zeros_rhs = jnp.zeros((cs, 2 * d), BF16)
    chunks = range(n_chunks)
    rows = [slice(c * cs, (c + 1) * cs) for c in chunks]

    def column(gb, idx):
        return jnp.sum(jnp.where(lane_gb == idx, gb, 0.0), axis=-1, keepdims=True)

    q = [q_ref[0, r, :] for r in rows]
    k = [k_ref[0, r, :] for r in rows]
    gb = [gb_ref[0, r, :] for r in rows]
    gc = [[column(g, rep * pair + h) for h in range(rep)] for g in gb]
    beta = [[column(g, nh + rep * pair + h) for h in range(rep)] for g in gb]
    gc_row = [gr_ref[0, 0, c] for c in chunks]
    decay = [jnp.exp(jnp.where(ii >= jj, jnp.where(first, gc[c][0], gc[c][1]) - gc_row[c], -jnp.inf))
             for c in chunks]
    k16 = [x.astype(BF16) for x in k]
    kk16 = [jnp.concatenate([x, x], axis=0) for x in k16]
    a = [jnp.where(ii > jj, _dot_nt(k16[c], kk16[c]) * jnp.where(first, beta[c][0], beta[c][1]) * decay[c], 0.0)
         for c in chunks]
    for c in chunks:
        at_ref[0, 0, rows[c], :] = jnp.where(
            ii >= jj, _dot_nt(q[c].astype(BF16), kk16[c]) * decay[c], 0.0).astype(BF16)
    tinv = _unit_lower_inverse_pairs(a, eye, same_block, bd_mask)

    eg = [[jnp.exp(gc[c][h]) for h in range(rep)] for c in chunks]
    sol = []
    for c in chunks:
        rhs_hi, rhs_lo = [], []
        for h in range(rep):
            v = v_ref[0, rows[c], h * d:(h + 1) * d]
            hi, lo = _split(jnp.concatenate([v * beta[c][h], k[c] * (beta[c][h] * eg[c][h])], axis=-1))
            pad = [zeros_rhs] * rep
            rhs_hi.append(jnp.concatenate(pad[:h] + [hi] + pad[h + 1:], axis=1))
            rhs_lo.append(jnp.concatenate(pad[:h] + [lo] + pad[h + 1:], axis=1))
        rhs_hi = jnp.concatenate(rhs_hi, axis=0)
        rhs_lo = jnp.concatenate(rhs_lo, axis=0)
        sol.append(_mm3(tinv[c], rhs_hi, rhs_lo))

    for c in chunks:
        for h in range(rep):
            u_ref[0, rows[c], h * d:(h + 1) * d] = sol[c][:, 2 * d * h:2 * d * h + d]
            wq_ref[0, h, c, :cs, :] = sol[c][:, 2 * d * h + d:2 * d * (h + 1)].astype(BF16)
            wq_ref[0, h, c, cs:, :] = (q[c] * eg[c][h]).astype(BF16)
            g_last = gc_row[c][:, h * cs + cs - 1:(h + 1) * cs]
            kt_ref[0, h, c] = (k[c] * jnp.exp(g_last - gc[c][h])).T.astype(BF16)


def _gdn_local(qkv, gb, gc_rows, n_chunks):
    b, t, _ = qkv.shape
    nh = GDN_V_HEADS
    rep = GDN_V_HEADS // GDN_K_HEADS
    cs = GDN_CHUNK
    tt = n_chunks * cs
    d = GDN_HEAD_DIM
    return pl.pallas_call(
        functools.partial(_gdn_local_body, n_chunks=n_chunks),
        out_shape=[
            jax.ShapeDtypeStruct((b, t, GDN_VW), F32),
            jax.ShapeDtypeStruct((b, nh, t // cs, 2 * cs, d), BF16),
            jax.ShapeDtypeStruct((b, nh, t // cs, d, cs), BF16),
            jax.ShapeDtypeStruct((b, GDN_K_HEADS, t, rep * cs), BF16),
        ],
        grid=(b, GDN_K_HEADS, t // tt),
        in_specs=[
            pl.BlockSpec((1, tt, d), lambda bi, p, ti: (bi, ti, p)),
            pl.BlockSpec((1, tt, d), lambda bi, p, ti: (bi, ti, GDN_K_HEADS + p)),
            pl.BlockSpec((1, tt, rep * d), lambda bi, p, ti: (bi, ti, GDN_K_HEADS + p)),
            pl.BlockSpec((1, tt, 2 * nh), lambda bi, p, ti: (bi, ti, 0)),
            pl.BlockSpec((1, 1, n_chunks, 1, rep * cs), lambda bi, p, ti: (bi, p, ti, 0, 0)),
        ],
        out_specs=[
            pl.BlockSpec((1, tt, rep * d), lambda bi, p, ti: (bi, ti, p)),
            pl.BlockSpec((1, rep, n_chunks, 2 * cs, d), lambda bi, p, ti: (bi, p, ti, 0, 0)),
            pl.BlockSpec((1, rep, n_chunks, d, cs), lambda bi, p, ti: (bi, p, ti, 0, 0)),
            pl.BlockSpec((1, 1, tt, rep * cs), lambda bi, p, ti: (bi, p, ti, 0)),
        ],
        compiler_params=_params(("parallel", "parallel", "parallel")),
        name="gdn_local",
    )(qkv, qkv, qkv, gb, gc_rows)


def _gdn_scan_body(u_ref, wq_ref, kt_ref, at_ref, gr_ref, z_ref, g_ref, y_ref, s_ref, *, n_chunks, n_pairs):
    cs = GDN_CHUNK
    d = GDN_HEAD_DIM
    rep = GDN_V_HEADS // GDN_K_HEADS
    heads = range(rep * n_pairs)

    @pl.when(pl.program_id(2) == 0)
    def _():
        s_ref[...] = jnp.zeros_like(s_ref)

    zeros_v = jnp.zeros((cs, d), BF16)
    states = [s_ref[i] for i in heads]
    for c in range(n_chunks):
        rows = slice(c * cs, (c + 1) * cs)
        s16 = [s.astype(BF16) for s in states]
        both = [_dot(wq_ref[0, i, c], s16[i]) for i in heads]
        v_new = [(u_ref[0, rows, i * d:(i + 1) * d] - both[i][:cs]).astype(BF16) for i in heads]
        for p in range(n_pairs):
            gc_row = gr_ref[0, p, c]
            diag = []
            for h in range(rep):
                pad = [zeros_v] * rep
                diag.append(jnp.concatenate(pad[:h] + [v_new[rep * p + h]] + pad[h + 1:], axis=1))
            o_attn = _dot(at_ref[0, p, rows, :], jnp.concatenate(diag, axis=0))
            for h in range(rep):
                i = rep * p + h
                o = both[i][cs:] + o_attn[:, h * d:(h + 1) * d]
                z = z_ref[0, rows, i * d:(i + 1) * d]
                gate = z / (1.0 + jnp.exp2(z * (-LOG2E)))
                y_ref[0, rows, i * d:(i + 1) * d] = (o * _rms_scale(o) * g_ref[...] * gate).astype(BF16)
                g_last = gc_row[:, h * cs + cs - 1:(h + 1) * cs]
                states[i] = states[i] * jnp.exp(g_last) + _dot(kt_ref[0, i, c], v_new[i])
    for i in heads:
        s_ref[i] = states[i]


def _gdn_scan(u, wq, kt, attn, gc_rows, z, gain, n_chunks, n_pairs):
    b, t, _ = u.shape
    rep = GDN_V_HEADS // GDN_K_HEADS
    cs = GDN_CHUNK
    tt = n_chunks * cs
    d = GDN_HEAD_DIM
    nhs = rep * n_pairs
    return pl.pallas_call(
        functools.partial(_gdn_scan_body, n_chunks=n_chunks, n_pairs=n_pairs),
        out_shape=jax.ShapeDtypeStruct((b, t, GDN_VW), BF16),
        grid=(b, GDN_K_HEADS // n_pairs, t // tt),
        in_specs=[
            pl.BlockSpec((1, tt, nhs * d), lambda bi, g, ti: (bi, ti, g)),
            pl.BlockSpec((1, nhs, n_chunks, 2 * cs, d), lambda bi, g, ti: (bi, g, ti, 0, 0)),
            pl.BlockSpec((1, nhs, n_chunks, d, cs), lambda bi, g, ti: (bi, g, ti, 0, 0)),
            pl.BlockSpec((1, n_pairs, tt, rep * cs), lambda bi, g, ti: (bi, g, ti, 0)),
            pl.BlockSpec((1, n_pairs, n_chunks, 1, rep * cs), lambda bi, g, ti: (bi, g, ti, 0, 0)),
            pl.BlockSpec((1, tt, nhs * d), lambda bi, g, ti: (bi, ti, g)),
            pl.BlockSpec((1, d), lambda bi, g, ti: (0, 0)),
        ],
        out_specs=pl.BlockSpec((1, tt, nhs * d), lambda bi, g, ti: (bi, ti, g)),
        scratch_shapes=[pltpu.VMEM((nhs, d, d), F32)],
        compiler_params=_params(("parallel", "parallel", "arbitrary")),
        name="gdn_scan",
    )(u, wq, kt, attn, gc_rows, z, gain.reshape(1, d))


def _attention_mixer(h, b, t, gain, w_in, q_gain, k_gain, sinks, w_out, tm):
    d = h.shape[1]
    swa_w = SWA_QW + 2 * SWA_KVW
    outs = ((SB_W, BF16, HEAD_DIM ** -0.5 * LOG2E), (2 * SB_W, BF16, None), (swa_w, F32, None))
    sb_q, sb_kv, swa = _norm_proj(h, gain, w_in.astype(BF16), outs, tm)
    a_out = _sb_attention(sb_q.reshape(b, t, SB_W), sb_kv.reshape(b, t, 2 * SB_W),
                          min(SB_BLOCK, t), SB_Q_BLOCKS if t >= SB_Q_BLOCKS * SB_BLOCK else 1,
                          SB_PAIRS).reshape(b * t, SB_W)
    b_out = _swa_attention(swa.reshape(b, t, swa_w), q_gain, k_gain, sinks,
                           min(SWA_TILE_BLOCKS, t // WINDOW)).reshape(b * t, SWA_QW)
    return [a_out, b_out], w_out.astype(BF16)


def _gdn_mixer(h, b, t, gain, w_in, conv_w, a_log, dt_bias, out_gain, w_out, tm):
    nh = GDN_V_HEADS
    ba_w = w_in.shape[1] - GDN_CONV_W - GDN_VW
    assert ba_w == 2 * nh
    qkv, z, gb = _gdn_in(h.reshape(b, t, -1), gain, w_in.astype(BF16), conv_w, a_log, dt_bias,
                         min(GDN_IN_TILE, t))
    rep = GDN_V_HEADS // GDN_K_HEADS
    gc_rows = gb[:, :, :nh].reshape(b, t // GDN_CHUNK, GDN_CHUNK, GDN_K_HEADS, rep)
    gc_rows = jnp.transpose(gc_rows, (0, 3, 1, 4, 2)).reshape(
        b, GDN_K_HEADS, t // GDN_CHUNK, 1, rep * GDN_CHUNK)
    u, wq, kt, attn = _gdn_local(qkv, gb, gc_rows, min(GDN_LOCAL_CHUNKS, t // GDN_CHUNK))
    y = _gdn_scan(u, wq, kt, attn, gc_rows, z, out_gain, min(GDN_SCAN_CHUNKS, t // GDN_CHUNK), GDN_SCAN_PAIRS)
    return [y.reshape(b * t, GDN_VW)], w_out.astype(BF16)


def kernel(x, p, ffn_norm, ffn_w_gate, ffn_w_up, ffn_w_down, mix_norm, att_w_in, att_q_norm, att_k_norm, att_sinks, att_w_out, gdn_w_in, gdn_conv_w, gdn_a_log, gdn_dt_bias, gdn_out_norm, gdn_w_out, ple_norm, ple_w_gate, ple_w_proj):
    b, t, d = x.shape
    n = b * t
    tm = min(TOKEN_TILE, n)
    depth = p.shape[0]
    h = x.reshape(n, d)
    wg, wu, wd = ffn_w_gate.astype(BF16), ffn_w_up.astype(BF16), ffn_w_down.astype(BF16)
    for i in range(depth):
        j = i // 2
        h = _ffn(h, ffn_norm[i, 0], wg, wu, wd, i, 0, tm)
        if i % 2 == 0:
            mix = _attention_mixer(h, b, t, mix_norm[i], att_w_in[j], att_q_norm[j], att_k_norm[j],
                                   att_sinks[j], att_w_out[j], tm)
        else:
            mix = _gdn_mixer(h, b, t, mix_norm[i], gdn_w_in[j], gdn_conv_w[j], gdn_a_log[j],
                             gdn_dt_bias[j], gdn_out_norm[j], gdn_w_out[j], tm)
        ple = (p.reshape(depth, n, -1), ple_norm[i], ple_w_gate[i].astype(BF16), ple_w_proj[i].astype(BF16))
        h = _ffn(h, ffn_norm[i, 1], wg, wu, wd, i, 1, tm, mix=mix, ple=ple)
    return h.reshape(b, t, d)
```

```python
import functools

import jax
import jax.numpy as jnp
from jax import lax
from jax.experimental import pallas as pl
from jax.experimental.pallas import tpu as pltpu

F32 = jnp.float32
BF16 = jnp.bfloat16

EPS = 1e-6
LOG2E = 1.4426950408889634
HEAD_DIM = 64
SB_HEADS = 8
SWA_HEADS = 8
SWA_KV_HEADS = 2
WINDOW = 128
GDN_K_HEADS = 8
GDN_V_HEADS = 16
GDN_HEAD_DIM = 128
GDN_CONV = 4
GDN_CHUNK = 64
SB_W = SB_HEADS * HEAD_DIM
SWA_QW = SWA_HEADS * HEAD_DIM
SWA_KVW = SWA_KV_HEADS * HEAD_DIM
GDN_KW = GDN_K_HEADS * GDN_HEAD_DIM
GDN_VW = GDN_V_HEADS * GDN_HEAD_DIM
GDN_CONV_W = 2 * GDN_KW + GDN_VW

LANES = 128
VMEM_LIMIT = 56 * 1024 * 1024

TOKEN_TILE = 512
FF_CHUNK = 256
SB_BLOCK = 256
SB_Q_BLOCKS = 2
SB_CUMSUM_TERMS = 1
SB_PAIRS = 4
SWA_TILE_BLOCKS = 8
GDN_IN_TILE = 1024
GDN_IN_COLS = 1024
GDN_IN_SUB = 256
CONV_HALO = 16
GDN_LOCAL_CHUNKS = 16
GDN_SCAN_CHUNKS = 8
GDN_SCAN_PAIRS = 8
INV_BLOCK = 16


def _params(sem):
    return pltpu.CompilerParams(dimension_semantics=sem, vmem_limit_bytes=VMEM_LIMIT)


def _dot(a, b):
    return jnp.dot(a, b, preferred_element_type=F32)


def _dot_nt(a, b):
    return lax.dot_general(a, b, (((1,), (1,)), ((), ())), preferred_element_type=F32)


def _dot_f32(a, b):
    return jnp.dot(a, b, preferred_element_type=F32, precision=lax.Precision.HIGHEST)


def _rms_scale(x):
    return lax.rsqrt(jnp.mean(x * x, axis=-1, keepdims=True) + EPS)


def _sigmoid(x):
    return 1.0 / (1.0 + jnp.exp(-x))


def _emit_skewed(n, stages):
    for tick in range(n + len(stages) - 1):
        for s, stage in enumerate(stages):
            if 0 <= tick - s < n:
                stage(tick - s)


def _neg_abs(x):
    bits = lax.bitcast_convert_type(x, jnp.uint32) | jnp.uint32(0x80000000)
    return lax.bitcast_convert_type(bits, F32)


def _softplus(x):
    return jnp.maximum(x, 0.0) + jnp.log1p(jnp.exp(-jnp.abs(x)))


def _ffn_body(*refs, n_chunks, fc, n_mix, ple):
    refs = list(refs)
    x_ref, g_ref, wg_ref, wu_ref, wd_ref = refs[:5]
    refs = refs[5:]
    mix_refs, refs = refs[:n_mix], refs[n_mix:]
    if n_mix:
        wmix_ref, refs = refs[0], refs[1:]
    if ple:
        p_ref, pg_ref, pwg_ref, pwp_ref = refs[:4]
        refs = refs[4:]
    o_ref, acc_ref = refs

    x = x_ref[...]
    start = 0
    for m_ref in mix_refs:
        width = m_ref.shape[1]
        x = x + _dot(m_ref[...].astype(BF16), wmix_ref[start:start + width, :])
        start += width
    xn = (x * _rms_scale(x) * g_ref[...]).astype(BF16)
    for c in range(n_chunks):
        cols = slice(c * fc, (c + 1) * fc)
        gate = _dot(xn, wg_ref[:, cols])
        up = _dot(xn, wu_ref[:, cols])
        act = (gate * _sigmoid(gate) * up).astype(BF16)
        part = _dot(act, wd_ref[cols, :])
        if c == 0:
            acc_ref[...] = part
        else:
            acc_ref[...] += part
    y = x + 0.5 * acc_ref[...]
    if ple:
        yn = (y * _rms_scale(y) * pg_ref[...]).astype(BF16)
        y = y + _sigmoid(_dot(yn, pwg_ref[...])) * _dot(p_ref[0].astype(BF16), pwp_ref[...])
    o_ref[...] = y


def _ffn(h, gain, wg, wu, wd, layer, half, tm, mix=None, ple=None):
    n, d = h.shape
    f = wg.shape[-1]
    fc = FF_CHUNK if f % FF_CHUNK == 0 else f
    const = lambda i: (0, 0)
    row = lambda i: (i, 0)
    pick = lambda i: (layer, half, 0, 0)
    whole = lambda a: pl.BlockSpec(a.shape, const, pipeline_mode=pl.Buffered(1))
    args = [h, gain.reshape(1, d), wg, wu, wd]
    in_specs = [
        pl.BlockSpec((tm, d), row),
        pl.BlockSpec((1, d), const),
        pl.BlockSpec((None, None, d, f), pick, pipeline_mode=pl.Buffered(1)),
        pl.BlockSpec((None, None, d, f), pick, pipeline_mode=pl.Buffered(1)),
        pl.BlockSpec((None, None, f, d), pick, pipeline_mode=pl.Buffered(1)),
    ]
    n_mix = 0
    if mix is not None:
        xs, w_mix = mix
        assert sum(x.shape[1] for x in xs) == w_mix.shape[0]
        n_mix = len(xs)
        args += list(xs) + [w_mix]
        in_specs += [pl.BlockSpec((tm, x.shape[1]), row) for x in xs] + [whole(w_mix)]
    if ple is not None:
        p, p_gain, w_gate, w_proj = ple
        args += [p, p_gain.reshape(1, d), w_gate, w_proj]
        in_specs += [pl.BlockSpec((1, tm, p.shape[2]), lambda i: (layer, i, 0)),
                     pl.BlockSpec((1, d), const), whole(w_gate), whole(w_proj)]
    return pl.pallas_call(
        functools.partial(_ffn_body, n_chunks=f // fc, fc=fc, n_mix=n_mix, ple=ple is not None),
        out_shape=jax.ShapeDtypeStruct((n, d), F32),
        grid=(n // tm,),
        in_specs=in_specs,
        out_specs=pl.BlockSpec((tm, d), row),
        scratch_shapes=[pltpu.VMEM((tm, d), F32)],
        compiler_params=_params(("parallel",)),
        name="ffn",
    )(*args)


def _norm_proj_body(x_ref, g_ref, w_ref, *o_refs, outs):
    x = x_ref[...]
    xn = (x * _rms_scale(x) * g_ref[...]).astype(BF16)
    start = 0
    for o_ref, (width, _, scale) in zip(o_refs, outs):
        y = _dot(xn, w_ref[:, start:start + width])
        o_ref[...] = (y if scale is None else y * scale).astype(o_ref.dtype)
        start += width


def _norm_proj(h, gain, w, outs, tm):
    n, d = h.shape
    widths = [o[0] for o in outs]
    assert sum(widths) == w.shape[1]
    return pl.pallas_call(
        functools.partial(_norm_proj_body, outs=outs),
        out_shape=[jax.ShapeDtypeStruct((n, width), dtype) for width, dtype, _ in outs],
        grid=(n // tm,),
        in_specs=[
            pl.BlockSpec((tm, d), lambda i: (i, 0)),
            pl.BlockSpec((1, d), lambda i: (0, 0)),
            pl.BlockSpec(w.shape, lambda i: (0, 0), pipeline_mode=pl.Buffered(1)),
        ],
        out_specs=[pl.BlockSpec((tm, width), lambda i: (i, 0)) for width in widths],
        compiler_params=_params(("parallel",)),
        name="norm_proj",
    )(h, gain.reshape(1, d), w)


def _sb_body(q_ref, k_ref, v_ref, o_ref, acc_ref, *, blk, qsub, n_pairs):
    qi = pl.program_id(2)
    per = LANES // HEAD_DIM
    assert per == 2
    tq = qsub * blk
    later = jnp.where(lax.broadcasted_iota(jnp.int32, (blk, blk), 0)
                      >= lax.broadcasted_iota(jnp.int32, (blk, blk), 1), 1.0, 0.0).astype(BF16)
    later2 = jnp.concatenate([later, later], axis=0)
    first = lax.broadcasted_iota(jnp.int32, (tq, LANES), 1) < HEAD_DIM
    heads = [(p, h) for p in range(n_pairs) for h in range(per)]
    q = []
    for p, h in heads:
        qp = q_ref[0, :, p * LANES:(p + 1) * LANES]
        q.append(jnp.where(first if h == 0 else ~first, qp, jnp.zeros_like(qp)))

    def block(j, carry, diag):
        start = pl.multiple_of(j * blk, blk)
        rows = pl.ds(start, blk)
        r0 = 0 if diag is None else diag * blk
        causal = None
        if diag is not None:
            causal = (lax.broadcasted_iota(jnp.int32, (tq - r0, blk), 1)
                      < lax.broadcasted_iota(jnp.int32, (tq - r0, blk), 0))
        n = len(heads)
        z, split, rc, w, pv = [None] * n, [None] * n, [None] * n, [None] * n, [None] * n

        def scores(i):
            p = heads[i][0]
            z[i] = _dot_nt(q[i][r0:], k_ref[0, rows, p * LANES:(p + 1) * LANES])

        def keep_terms(i):
            sp = jnp.maximum(z[i], 0.0) + jnp.log(1.0 + jnp.exp2(_neg_abs(z[i]))) * LOG2E
            if causal is not None:
                sp = jnp.where(causal, sp, 0.0)
            hi = sp.astype(BF16)
            if SB_CUMSUM_TERMS == 1:
                split[i] = hi
            else:
                split[i] = jnp.concatenate([hi, (sp - hi.astype(F32)).astype(BF16)], axis=1)

        def cumulate(i):
            rc[i] = _dot(split[i], later if SB_CUMSUM_TERMS == 1 else later2)

        def weights(i):
            x = z[i] - rc[i]
            if carry is not None:
                x = x - carry[i]
            x = jnp.exp2(x)
            if causal is not None:
                x = jnp.where(causal, x, 0.0)
            w[i] = x.astype(BF16)

        def values(i):
            p = heads[i][0]
            pv[i] = _dot(w[i], v_ref[0, rows, p * LANES:(p + 1) * LANES])

        _emit_skewed(n, (scores, keep_terms, cumulate, weights, values))
        return pv, [r[:, 0:1] for r in rc]

    carry = None
    for diag in reversed(range(qsub)):
        r0 = diag * blk
        pv, tot = block(qi * qsub + diag, None if carry is None else [c[r0:] for c in carry], diag)
        if r0:
            tot = [jnp.concatenate([jnp.zeros((r0, 1), F32), t], axis=0) for t in tot]
        for i, x in enumerate(pv):
            if carry is None:
                if r0:
                    acc_ref[i, :r0, :] = jnp.zeros((r0, LANES), F32)
                acc_ref[i, r0:, :] = x
            else:
                acc_ref[i, r0:, :] += x
        carry = tot if carry is None else [c + t for c, t in zip(carry, tot)]

    def step(it, carry):
        pv, tot = block(qi * qsub - 1 - it, carry, None)
        for i, x in enumerate(pv):
            acc_ref[i] += x
        return tuple(c + t for c, t in zip(carry, tot))

    lax.fori_loop(0, qi * qsub, step, tuple(carry))
    for p in range(n_pairs):
        o_ref[0, :, p * LANES:(p + 1) * LANES] = jnp.where(first, acc_ref[per * p], acc_ref[per * p + 1])


def _sb_attention(q, kv, blk, qsub, n_pairs):
    b, t, _ = q.shape
    width = n_pairs * LANES
    groups = SB_W // width
    tq = qsub * blk
    return pl.pallas_call(
        functools.partial(_sb_body, blk=blk, qsub=qsub, n_pairs=n_pairs),
        out_shape=jax.ShapeDtypeStruct((b, t, SB_W), F32),
        grid=(b, groups, t // tq),
        in_specs=[
            pl.BlockSpec((1, tq, width), lambda bi, g, qi: (bi, qi, g)),
            pl.BlockSpec((1, t, width), lambda bi, g, qi: (bi, 0, g)),
            pl.BlockSpec((1, t, width), lambda bi, g, qi: (bi, 0, groups + g)),
        ],
        out_specs=pl.BlockSpec((1, tq, width), lambda bi, g, qi: (bi, qi, g)),
        scratch_shapes=[pltpu.VMEM((2 * n_pairs, tq, LANES), F32)],
        compiler_params=_params(("parallel", "parallel", "arbitrary")),
        name="sb_attention",
    )(q, kv, kv)


def _half_rms_scale(x, first):
    sq = x * x
    ms0 = jnp.sum(jnp.where(first, sq, 0.0), axis=-1, keepdims=True) * (1.0 / HEAD_DIM)
    ms1 = jnp.sum(jnp.where(first, 0.0, sq), axis=-1, keepdims=True) * (1.0 / HEAD_DIM)
    return jnp.where(first, lax.rsqrt(ms0 + EPS), lax.rsqrt(ms1 + EPS))


def _swa_body(sink_ref, q_ref, kvc_ref, kvp_ref, qg_ref, kg_ref, o_ref, *, n_blocks, slopes):
    tile = pl.program_id(1)
    w = WINDOW
    group = SWA_HEADS // SWA_KV_HEADS
    per = LANES // HEAD_DIM
    assert per == 2 and SWA_KVW == LANES
    first = lax.broadcasted_iota(jnp.int32, (1, LANES), 1) < HEAD_DIM

    def both_halves(x):
        swapped = pltpu.roll(x, HEAD_DIM, axis=1)
        return jnp.where(first, x, swapped), jnp.where(first, swapped, x)

    kv = jnp.concatenate([kvp_ref[0], kvc_ref[0]], axis=0)
    k = kv[:, :SWA_KVW]
    k = k * _half_rms_scale(k, first) * kg_ref[...]
    k_dup = [x.astype(BF16) for x in both_halves(k)]
    v_dup = [x.astype(BF16) for x in both_halves(kv[:, SWA_KVW:])]

    q_heads = []
    for g2 in range(SWA_QW // LANES):
        q = q_ref[0, :, g2 * LANES:(g2 + 1) * LANES]
        q = q * _half_rms_scale(q, first) * (qg_ref[...] * (HEAD_DIM ** -0.5 * LOG2E))
        q_heads.append(jnp.where(first, q, 0.0).astype(BF16))
        q_heads.append(jnp.where(first, 0.0, q).astype(BF16))

    qpos = lax.broadcasted_iota(jnp.int32, (w, 2 * w), 0)
    kpos = lax.broadcasted_iota(jnp.int32, (w, 2 * w), 1)
    dist = qpos + w - kpos
    band = (dist >= 0) & (dist < w)
    distf = dist.astype(F32)
    has_prev = (tile > 0) | (kpos >= w)
    bias, sink = [], []
    for kvh in range(SWA_KV_HEADS):
        heads = range(kvh * group, (kvh + 1) * group)
        bias.append(jnp.concatenate(
            [jnp.where(band, (-slopes[h] * LOG2E) * distf, -jnp.inf) for h in heads], axis=0))
        sink.append(jnp.concatenate(
            [jnp.full((w, 1), sink_ref[h] * LOG2E, F32) for h in heads], axis=0))

    units = [(j, kvh) for j in range(n_blocks) for kvh in range(SWA_KV_HEADS)]
    n = len(units)
    s, e, denom, res = [None] * n, [None] * n, [None] * n, [None] * n

    def scores(i):
        j, kvh = units[i]
        q4 = jnp.concatenate([q_heads[kvh * group + g][j * w:(j + 1) * w] for g in range(group)], axis=0)
        s[i] = _dot_nt(q4, k_dup[kvh][j * w:(j + 2) * w]) + bias[kvh]
        if j == 0:
            s[i] = jnp.where(jnp.concatenate([has_prev] * group, axis=0), s[i], -jnp.inf)

    def softmax(i):
        kvh = units[i][1]
        m = jnp.maximum(jnp.max(s[i], axis=-1, keepdims=True), sink[kvh])
        p = jnp.exp2(s[i] - m)
        denom[i] = jnp.sum(p, axis=-1, keepdims=True) + jnp.exp2(sink[kvh] - m)
        e[i] = p.astype(BF16)

    def values(i):
        j, kvh = units[i]
        res[i] = _dot(e[i], v_dup[kvh][j * w:(j + 2) * w]) / denom[i]

    def store(i):
        j, kvh = units[i]
        for g2 in range(group // per):
            piece = [res[i][(per * g2 + h) * w:(per * g2 + h + 1) * w] for h in range(per)]
            lanes = (kvh * (group // per) + g2) * LANES
            o_ref[0, j * w:(j + 1) * w, lanes:lanes + LANES] = jnp.where(first, piece[0], piece[1])

    _emit_skewed(n, (scores, softmax, values, store))


def _swa_attention(proj, q_gain, k_gain, sinks, n_blocks):
    b, t, _ = proj.shape
    w = WINDOW
    tq = n_blocks * w
    kv_blk = SWA_QW // (2 * SWA_KVW)
    slopes = tuple(float(2.0 ** (-8.0 * (i + 1) / SWA_HEADS)) for i in range(SWA_HEADS))
    gain2 = lambda g: jnp.concatenate([g, g]).reshape(1, LANES)
    return pl.pallas_call(
        functools.partial(_swa_body, n_blocks=n_blocks, slopes=slopes),
        out_shape=jax.ShapeDtypeStruct((b, t, SWA_QW), F32),
        grid=(b, t // tq),
        in_specs=[
            pl.BlockSpec(memory_space=pltpu.SMEM),
            pl.BlockSpec((1, tq, SWA_QW), lambda bi, i: (bi, i, 0)),
            pl.BlockSpec((1, tq, 2 * SWA_KVW), lambda bi, i: (bi, i, kv_blk)),
            pl.BlockSpec((1, w, 2 * SWA_KVW), lambda bi, i: (bi, jnp.maximum(i * n_blocks - 1, 0), kv_blk)),
            pl.BlockSpec((1, LANES), lambda bi, i: (0, 0)),
            pl.BlockSpec((1, LANES), lambda bi, i: (0, 0)),
        ],
        out_specs=pl.BlockSpec((1, tq, SWA_QW), lambda bi, i: (bi, i, 0)),
        compiler_params=_params(("parallel", "parallel")),
        name="swa_attention",
    )(sinks, proj, proj, proj, gain2(q_gain), gain2(k_gain))


def _gdn_in_body(x_ref, halo_ref, g_ref, w_ref, wba_ref, cw_ref, alog_ref, dtb_ref,
                 qkv_ref, z_ref, gb_ref, xn_ref, pre_ref, *, tm, cols, sub):
    ti = pl.program_id(1)
    step = pl.program_id(2)
    conv_steps = GDN_CONV_W // cols
    qk_steps = GDN_KW // cols
    nh = GDN_V_HEADS

    @pl.when(step == 0)
    def _():
        xh = halo_ref[0]
        x = x_ref[0]
        xn_ref[0:CONV_HALO, :] = (xh * _rms_scale(xh) * g_ref[...]).astype(BF16)
        xn_ref[CONV_HALO:, :] = (x * _rms_scale(x) * g_ref[...]).astype(BF16)
        ba = _dot(xn_ref[CONV_HALO:, :], wba_ref[...])
        beta = _sigmoid(ba[:, :nh])
        g = -jnp.exp(alog_ref[...]) * _softplus(ba[:, nh:2 * nh] + dtb_ref[...])
        r = lax.broadcasted_iota(jnp.int32, (GDN_CHUNK, GDN_CHUNK), 0)
        c = lax.broadcasted_iota(jnp.int32, (GDN_CHUNK, GDN_CHUNK), 1)
        lower = jnp.where(r >= c, 1.0, 0.0).astype(F32)
        for chunk in range(tm // GDN_CHUNK):
            rows = slice(chunk * GDN_CHUNK, (chunk + 1) * GDN_CHUNK)
            gb_ref[0, rows, :] = jnp.concatenate([_dot_f32(lower, g[rows]), beta[rows]], axis=-1)

    def conv_step(normalise):
        def project(i):
            pre = _dot(xn_ref[...], w_ref[:, i * sub:(i + 1) * sub])
            pre_ref[i, 0:CONV_HALO, :] = jnp.where(ti > 0, pre[:CONV_HALO], 0.0)
            pre_ref[i, CONV_HALO:, :] = pre[CONV_HALO:]

        def conv(i):
            ext = pre_ref[i]
            y = None
            for tap in range(GDN_CONV):
                back = GDN_CONV - 1 - tap
                rows = ext if back == 0 else pltpu.roll(ext, back, axis=0)
                term = rows[CONV_HALO:] * cw_ref[tap:tap + 1, i * sub:(i + 1) * sub]
                y = term if y is None else y + term
            y = y / (1.0 + jnp.exp2(y * (-LOG2E)))
            if not normalise:
                qkv_ref[0, :, i * sub:(i + 1) * sub] = y
                return
            q_scale = jnp.where(step < qk_steps, GDN_HEAD_DIM ** -0.5, 1.0)
            for head in range(sub // GDN_HEAD_DIM):
                lanes = slice(head * GDN_HEAD_DIM, (head + 1) * GDN_HEAD_DIM)
                yh = y[:, lanes]
                inv = lax.rsqrt(jnp.sum(yh * yh, axis=-1, keepdims=True) + EPS) * q_scale
                qkv_ref[0, :, i * sub + head * GDN_HEAD_DIM:i * sub + (head + 1) * GDN_HEAD_DIM] = yh * inv

        _emit_skewed(cols // sub, (project, conv))

    pl.when(step < 2 * qk_steps)(functools.partial(conv_step, True))
    pl.when((step >= 2 * qk_steps) & (step < conv_steps))(functools.partial(conv_step, False))

    @pl.when(step >= conv_steps)
    def _():
        z_ref[0] = _dot(xn_ref[CONV_HALO:, :], w_ref[...])


def _gdn_in(h3, gain, w, conv_w, a_log, dt_bias, tm):
    b, t, d = h3.shape
    nh = GDN_V_HEADS
    cols = GDN_IN_COLS
    conv_steps = GDN_CONV_W // cols
    steps = (GDN_CONV_W + GDN_VW) // cols
    w_ba = w[:, GDN_CONV_W + GDN_VW:]
    return pl.pallas_call(
        functools.partial(_gdn_in_body, tm=tm, cols=cols, sub=GDN_IN_SUB),
        out_shape=[
            jax.ShapeDtypeStruct((b, t, GDN_CONV_W), F32),
            jax.ShapeDtypeStruct((b, t, GDN_VW), F32),
            jax.ShapeDtypeStruct((b, t, 2 * nh), F32),
        ],
        grid=(b, t // tm, steps),
        in_specs=[
            pl.BlockSpec((1, tm, d), lambda bi, ti, s: (bi, ti, 0)),
            pl.BlockSpec((1, CONV_HALO, d),
                         lambda bi, ti, s: (bi, jnp.maximum(ti * (tm // CONV_HALO) - 1, 0), 0)),
            pl.BlockSpec((1, d), lambda bi, ti, s: (0, 0)),
            pl.BlockSpec((d, cols), lambda bi, ti, s: (0, s)),
            pl.BlockSpec(w_ba.shape, lambda bi, ti, s: (0, 0)),
            pl.BlockSpec((GDN_CONV, cols), lambda bi, ti, s: (0, jnp.minimum(s, conv_steps - 1))),
            pl.BlockSpec((1, nh), lambda bi, ti, s: (0, 0)),
            pl.BlockSpec((1, nh), lambda bi, ti, s: (0, 0)),
        ],
        out_specs=[
            pl.BlockSpec((1, tm, cols), lambda bi, ti, s: (bi, ti, jnp.minimum(s, conv_steps - 1))),
            pl.BlockSpec((1, tm, cols), lambda bi, ti, s: (bi, ti, jnp.maximum(s - conv_steps, 0))),
            pl.BlockSpec((1, tm, 2 * nh), lambda bi, ti, s: (bi, ti, 0)),
        ],
        scratch_shapes=[
            pltpu.VMEM((CONV_HALO + tm, d), BF16),
            pltpu.VMEM((cols // GDN_IN_SUB, CONV_HALO + tm, GDN_IN_SUB), F32),
        ],
        compiler_params=_params(("parallel", "parallel", "arbitrary")),
        name="gdn_in",
    )(h3, h3, gain.reshape(1, d), w, w_ba, conv_w, a_log.reshape(1, nh), dt_bias.reshape(1, nh))


def _split(x):
    hi = x.astype(BF16)
    return hi, (x - hi.astype(F32)).astype(BF16)


def _block_diag(y, mask):
    return jnp.where(mask, jnp.concatenate([y, y], axis=0), jnp.zeros((), y.dtype))


def _mm3(x, y_hi, y_lo):
    x_hi, x_lo = _split(x)
    return _dot(jnp.concatenate([x_hi, x_hi, x_lo], axis=1), jnp.concatenate([y_hi, y_lo, y_hi], axis=0))


def _pair_mm(x, y, mask):
    y_hi, y_lo = _split(y)
    return _mm3(x, _block_diag(y_hi, mask), _block_diag(y_lo, mask))


def _unit_lower_inverse_pairs(a_list, eye, same_block, bd_mask):
    mm = lambda xs, ys: [_pair_mm(x, y, bd_mask) for x, y in zip(xs, ys)]
    add = lambda xs, ys: [x + y for x, y in zip(xs, ys)]
    c = a_list[0].shape[0]
    ad = [jnp.where(same_block, a, 0.0) for a in a_list]
    lo = [a - d for a, d in zip(a_list, ad)]
    dinv = [eye - d for d in ad]
    power = ad
    span = 2
    while span < INV_BLOCK:
        power = mm(power, power)
        dinv = add(dinv, mm(dinv, power))
        span *= 2
    n = mm(dinv, lo)
    m = [eye - x for x in n]
    power = n
    span = 2
    while span < c // INV_BLOCK:
        power = mm(power, power)
        m = add(m, mm(m, power))
        span *= 2
    return mm(m, dinv)


def _gdn_local_body(q_ref, k_ref, v_ref, gb_ref, gr_ref, u_ref, wq_ref, kt_ref, at_ref, *, n_chunks):
    pair = pl.program_id(1)
    cs = GDN_CHUNK
    nh = GDN_V_HEADS
    d = GDN_HEAD_DIM
    rep = GDN_V_HEADS // GDN_K_HEADS
    assert rep == 2 and rep * cs == LANES

    ii = lax.broadcasted_iota(jnp.int32, (cs, rep * cs), 0)
    lane = lax.broadcasted_iota(jnp.int32, (cs, rep * cs), 1)
    jj = lane % cs
    first = lane < cs
    eye = jnp.where(ii == jj, 1.0, 0.0).astype(F32)
    same_block = (ii // INV_BLOCK) == (jj // INV_BLOCK)
    r2 = lax.broadcasted_iota(jnp.int32, (rep * cs, rep * cs), 0)
    c2 = lax.broadcasted_iota(jnp.int32, (rep * cs, rep * cs), 1)
    bd_mask = (r2 // cs) == (c2 // cs)
    lane_gb = lax.broadcasted_iota(jnp.int32, (cs, 2 * nh), 1)
    zeros_rhs = jnp.zeros((cs, 2 * d), BF16)
    chunks = range(n_chunks)
    rows = [slice(c * cs, (c + 1) * cs) for c in chunks]

    def column(gb, idx):
        return jnp.sum(jnp.where(lane_gb == idx, gb, 0.0), axis=-1, keepdims=True)

    q = [q_ref[0, r, :] for r in rows]
    k = [k_ref[0, r, :] for r in rows]
    gb = [gb_ref[0, r, :] for r in rows]
    gc = [[column(g, rep * pair + h) for h in range(rep)] for g in gb]
    beta = [[column(g, nh + rep * pair + h) for h in range(rep)] for g in gb]
    gc_row = [gr_ref[0, 0, c] for c in chunks]
    decay = [jnp.exp(jnp.where(ii >= jj, jnp.where(first, gc[c][0], gc[c][1]) - gc_row[c], -jnp.inf))
             for c in chunks]
    k16 = [x.astype(BF16) for x in k]
    kk16 = [jnp.concatenate([x, x], axis=0) for x in k16]
    a = [jnp.where(ii > jj, _dot_nt(k16[c], kk16[c]) * jnp.where(first, beta[c][0], beta[c][1]) * decay[c], 0.0)
         for c in chunks]
    for c in chunks:
        at_ref[0, 0, rows[c], :] = jnp.where(
            ii >= jj, _dot_nt(q[c].astype(BF16), kk16[c]) * decay[c], 0.0).astype(BF16)
    tinv = _unit_lower_inverse_pairs(a, eye, same_block, bd_mask)

    eg = [[jnp.exp(gc[c][h]) for h in range(rep)] for c in chunks]
    sol = []
    for c in chunks:
        rhs_hi, rhs_lo = [], []
        for h in range(rep):
            v = v_ref[0, rows[c], h * d:(h + 1) * d]
            hi, lo = _split(jnp.concatenate([v * beta[c][h], k[c] * (beta[c][h] * eg[c][h])], axis=-1))
            pad = [zeros_rhs] * rep
            rhs_hi.append(jnp.concatenate(pad[:h] + [hi] + pad[h + 1:], axis=1))
            rhs_lo.append(jnp.concatenate(pad[:h] + [lo] + pad[h + 1:], axis=1))
        rhs_hi = jnp.concatenate(rhs_hi, axis=0)
        rhs_lo = jnp.concatenate(rhs_lo, axis=0)
        sol.append(_mm3(tinv[c], rhs_hi, rhs_lo))

    for c in chunks:
        for h in range(rep):
            u_ref[0, rows[c], h * d:(h + 1) * d] = sol[c][:, 2 * d * h:2 * d * h + d]
            wq_ref[0, h, c, :cs, :] = sol[c][:, 2 * d * h + d:2 * d * (h + 1)].astype(BF16)
            wq_ref[0, h, c, cs:, :] = (q[c] * eg[c][h]).astype(BF16)
            g_last = gc_row[c][:, h * cs + cs - 1:(h + 1) * cs]
            kt_ref[0, h, c] = (k[c] * jnp.exp(g_last - gc[c][h])).T.astype(BF16)


def _gdn_local(qkv, gb, gc_rows, n_chunks):
    b, t, _ = qkv.shape
    nh = GDN_V_HEADS
    rep = GDN_V_HEADS // GDN_K_HEADS
    cs = GDN_CHUNK
    tt = n_chunks * cs
    d = GDN_HEAD_DIM
    return pl.pallas_call(
        functools.partial(_gdn_local_body, n_chunks=n_chunks),
        out_shape=[
            jax.ShapeDtypeStruct((b, t, GDN_VW), F32),
            jax.ShapeDtypeStruct((b, nh, t // cs, 2 * cs, d), BF16),
            jax.ShapeDtypeStruct((b, nh, t // cs, d, cs), BF16),
            jax.ShapeDtypeStruct((b, GDN_K_HEADS, t, rep * cs), BF16),
        ],
        grid=(b, GDN_K_HEADS, t // tt),
        in_specs=[
            pl.BlockSpec((1, tt, d), lambda bi, p, ti: (bi, ti, p)),
            pl.BlockSpec((1, tt, d), lambda bi, p, ti: (bi, ti, GDN_K_HEADS + p)),
            pl.BlockSpec((1, tt, rep * d), lambda bi, p, ti: (bi, ti, GDN_K_HEADS + p)),
            pl.BlockSpec((1, tt, 2 * nh), lambda bi, p, ti: (bi, ti, 0)),
            pl.BlockSpec((1, 1, n_chunks, 1, rep * cs), lambda bi, p, ti: (bi, p, ti, 0, 0)),
        ],
        out_specs=[
            pl.BlockSpec((1, tt, rep * d), lambda bi, p, ti: (bi, ti, p)),
            pl.BlockSpec((1, rep, n_chunks, 2 * cs, d), lambda bi, p, ti: (bi, p, ti, 0, 0)),
            pl.BlockSpec((1, rep, n_chunks, d, cs), lambda bi, p, ti: (bi, p, ti, 0, 0)),
            pl.BlockSpec((1, 1, tt, rep * cs), lambda bi, p, ti: (bi, p, ti, 0)),
        ],
        compiler_params=_params(("parallel", "parallel", "parallel")),
        name="gdn_local",
    )(qkv, qkv, qkv, gb, gc_rows)


def _gdn_scan_body(u_ref, wq_ref, kt_ref, at_ref, gr_ref, z_ref, g_ref, y_ref, s_ref, *, n_chunks, n_pairs):
    cs = GDN_CHUNK
    d = GDN_HEAD_DIM
    rep = GDN_V_HEADS // GDN_K_HEADS
    heads = range(rep * n_pairs)

    @pl.when(pl.program_id(2) == 0)
    def _():
        s_ref[...] = jnp.zeros_like(s_ref)

    zeros_v = jnp.zeros((cs, d), BF16)
    states = [s_ref[i] for i in heads]
    for c in range(n_chunks):
        rows = slice(c * cs, (c + 1) * cs)
        s16 = [s.astype(BF16) for s in states]
        both = [_dot(wq_ref[0, i, c], s16[i]) for i in heads]
        v_new = [(u_ref[0, rows, i * d:(i + 1) * d] - both[i][:cs]).astype(BF16) for i in heads]
        for p in range(n_pairs):
            gc_row = gr_ref[0, p, c]
            diag = []
            for h in range(rep):
                pad = [zeros_v] * rep
                diag.append(jnp.concatenate(pad[:h] + [v_new[rep * p + h]] + pad[h + 1:], axis=1))
            o_attn = _dot(at_ref[0, p, rows, :], jnp.concatenate(diag, axis=0))
            for h in range(rep):
                i = rep * p + h
                o = both[i][cs:] + o_attn[:, h * d:(h + 1) * d]
                z = z_ref[0, rows, i * d:(i + 1) * d]
                gate = z / (1.0 + jnp.exp2(z * (-LOG2E)))
                y_ref[0, rows, i * d:(i + 1) * d] = (o * _rms_scale(o) * g_ref[...] * gate).astype(BF16)
                g_last = gc_row[:, h * cs + cs - 1:(h + 1) * cs]
                states[i] = states[i] * jnp.exp(g_last) + _dot(kt_ref[0, i, c], v_new[i])
    for i in heads:
        s_ref[i] = states[i]


def _gdn_scan(u, wq, kt, attn, gc_rows, z, gain, n_chunks, n_pairs):
    b, t, _ = u.shape
    rep = GDN_V_HEADS // GDN_K_HEADS
    cs = GDN_CHUNK
    tt = n_chunks * cs
    d = GDN_HEAD_DIM
    nhs = rep * n_pairs
    return pl.pallas_call(
        functools.partial(_gdn_scan_body, n_chunks=n_chunks, n_pairs=n_pairs),
        out_shape=jax.ShapeDtypeStruct((b, t, GDN_VW), BF16),
        grid=(b, GDN_K_HEADS // n_pairs, t // tt),
        in_specs=[
            pl.BlockSpec((1, tt, nhs * d), lambda bi, g, ti: (bi, ti, g)),
            pl.BlockSpec((1, nhs, n_chunks, 2 * cs, d), lambda bi, g, ti: (bi, g, ti, 0, 0)),
            pl.BlockSpec((1, nhs, n_chunks, d, cs), lambda bi, g, ti: (bi, g, ti, 0, 0)),
            pl.BlockSpec((1, n_pairs, tt, rep * cs), lambda bi, g, ti: (bi, g, ti, 0)),
            pl.BlockSpec((1, n_pairs, n_chunks, 1, rep * cs), lambda bi, g, ti: (bi, g, ti, 0, 0)),
            pl.BlockSpec((1, tt, nhs * d), lambda bi, g, ti: (bi, ti, g)),
            pl.BlockSpec((1, d), lambda bi, g, ti: (0, 0)),
        ],
        out_specs=pl.BlockSpec((1, tt, nhs * d), lambda bi, g, ti: (bi, ti, g)),
        scratch_shapes=[pltpu.VMEM((nhs, d, d), F32)],
        compiler_params=_params(("parallel", "parallel", "arbitrary")),
        name="gdn_scan",
    )(u, wq, kt, attn, gc_rows, z, gain.reshape(1, d))


def _attention_mixer(h, b, t, gain, w_in, q_gain, k_gain, sinks, w_out, tm):
    d = h.shape[1]
    swa_w = SWA_QW + 2 * SWA_KVW
    outs = ((SB_W, BF16, HEAD_DIM ** -0.5 * LOG2E), (2 * SB_W, BF16, None), (swa_w, F32, None))
    sb_q, sb_kv, swa = _norm_proj(h, gain, w_in.astype(BF16), outs, tm)
    a_out = _sb_attention(sb_q.reshape(b, t, SB_W), sb_kv.reshape(b, t, 2 * SB_W),
                          min(SB_BLOCK, t), SB_Q_BLOCKS if t >= SB_Q_BLOCKS * SB_BLOCK else 1,
                          SB_PAIRS).reshape(b * t, SB_W)
    b_out = _swa_attention(swa.reshape(b, t, swa_w), q_gain, k_gain, sinks,
                           min(SWA_TILE_BLOCKS, t // WINDOW)).reshape(b * t, SWA_QW)
    return [a_out, b_out], w_out.astype(BF16)


def _gdn_mixer(h, b, t, gain, w_in, conv_w, a_log, dt_bias, out_gain, w_out, tm):
    nh = GDN_V_HEADS
    ba_w = w_in.shape[1] - GDN_CONV_W - GDN_VW
    assert ba_w == 2 * nh
    qkv, z, gb = _gdn_in(h.reshape(b, t, -1), gain, w_in.astype(BF16), conv_w, a_log, dt_bias,
                         min(GDN_IN_TILE, t))
    rep = GDN_V_HEADS // GDN_K_HEADS
    gc_rows = gb[:, :, :nh].reshape(b, t // GDN_CHUNK, GDN_CHUNK, GDN_K_HEADS, rep)
    gc_rows = jnp.transpose(gc_rows, (0, 3, 1, 4, 2)).reshape(
        b, GDN_K_HEADS, t // GDN_CHUNK, 1, rep * GDN_CHUNK)
    u, wq, kt, attn = _gdn_local(qkv, gb, gc_rows, min(GDN_LOCAL_CHUNKS, t // GDN_CHUNK))
    y = _gdn_scan(u, wq, kt, attn, gc_rows, z, out_gain, min(GDN_SCAN_CHUNKS, t // GDN_CHUNK), GDN_SCAN_PAIRS)
    return [y.reshape(b * t, GDN_VW)], w_out.astype(BF16)


def kernel(x, p, ffn_norm, ffn_w_gate, ffn_w_up, ffn_w_down, mix_norm, att_w_in, att_q_norm, att_k_norm, att_sinks, att_w_out, gdn_w_in, gdn_conv_w, gdn_a_log, gdn_dt_bias, gdn_out_norm, gdn_w_out, ple_norm, ple_w_gate, ple_w_proj):
    b, t, d = x.shape
    n = b * t
    tm = min(TOKEN_TILE, n)
    depth = p.shape[0]
    h = x.reshape(n, d)
    wg, wu, wd = ffn_w_gate.astype(BF16), ffn_w_up.astype(BF16), ffn_w_down.astype(BF16)
    for i in range(depth):
        j = i // 2
        h = _ffn(h, ffn_norm[i, 0], wg, wu, wd, i, 0, tm)
        if i % 2 == 0:
            mix = _attention_mixer(h, b, t, mix_norm[i], att_w_in[j], att_q_norm[j], att_k_norm[j],
                                   att_sinks[j], att_w_out[j], tm)
        else:
            mix = _gdn_mixer(h, b, t, mix_norm[i], gdn_w_in[j], gdn_conv_w[j], gdn_a_log[j],
                             gdn_dt_bias[j], gdn_out_norm[j], gdn_w_out[j], tm)
        ple = (p.reshape(depth, n, -1), ple_norm[i], ple_w_gate[i].astype(BF16), ple_w_proj[i].astype(BF16))
        h = _ffn(h, ffn_norm[i, 1], wg, wu, wd, i, 1, tm, mix=mix, ple=ple)
    return h.reshape(b, t, d)
```

```python
import functools

import jax
import jax.numpy as jnp
from jax import lax
from jax.experimental import pallas as pl
from jax.experimental.pallas import tpu as pltpu

F32 = jnp.float32
BF16 = jnp.bfloat16

EPS = 1e-6
LOG2E = 1.4426950408889634
HEAD_DIM = 64
SB_HEADS = 8
SWA_HEADS = 8
SWA_KV_HEADS = 2
WINDOW = 128
GDN_K_HEADS = 8
GDN_V_HEADS = 16
GDN_HEAD_DIM = 128
GDN_CONV = 4
GDN_CHUNK = 64
SB_W = SB_HEADS * HEAD_DIM
SWA_QW = SWA_HEADS * HEAD_DIM
SWA_KVW = SWA_KV_HEADS * HEAD_DIM
GDN_KW = GDN_K_HEADS * GDN_HEAD_DIM
GDN_VW = GDN_V_HEADS * GDN_HEAD_DIM
GDN_CONV_W = 2 * GDN_KW + GDN_VW

LANES = 128
VMEM_LIMIT = 56 * 1024 * 1024

TOKEN_TILE = 512
FF_CHUNK = 256
SB_BLOCK = 256
SB_Q_BLOCKS = 2
SB_CUMSUM_TERMS = 1
SB_PAIRS = 4
SWA_TILE_BLOCKS = 8
GDN_IN_TILE = 1024
GDN_IN_COLS = 1024
GDN_IN_SUB = 256
CONV_HALO = 16
GDN_LOCAL_CHUNKS = 16
GDN_SCAN_CHUNKS = 8
GDN_SCAN_PAIRS = 8
INV_BLOCK = 16


def _params(sem):
    return pltpu.CompilerParams(dimension_semantics=sem, vmem_limit_bytes=VMEM_LIMIT)


def _dot(a, b):
    return jnp.dot(a, b, preferred_element_type=F32)


def _dot_nt(a, b):
    return lax.dot_general(a, b, (((1,), (1,)), ((), ())), preferred_element_type=F32)


def _dot_f32(a, b):
    return jnp.dot(a, b, preferred_element_type=F32, precision=lax.Precision.HIGHEST)


def _rms_scale(x):
    return lax.rsqrt(jnp.mean(x * x, axis=-1, keepdims=True) + EPS)


def _sigmoid(x):
    return 1.0 / (1.0 + jnp.exp(-x))


def _emit_skewed(n, stages):
    for tick in range(n + len(stages) - 1):
        for s, stage in enumerate(stages):
            if 0 <= tick - s < n:
                stage(tick - s)


def _neg_abs(x):
    bits = lax.bitcast_convert_type(x, jnp.uint32) | jnp.uint32(0x80000000)
    return lax.bitcast_convert_type(bits, F32)


def _softplus(x):
    return jnp.maximum(x, 0.0) + jnp.log1p(jnp.exp(-jnp.abs(x)))


def _ffn_body(*refs, n_chunks, fc, n_mix, ple):
    refs = list(refs)
    x_ref, g_ref, wg_ref, wu_ref, wd_ref = refs[:5]
    refs = refs[5:]
    mix_refs, refs = refs[:n_mix], refs[n_mix:]
    if n_mix:
        wmix_ref, refs = refs[0], refs[1:]
    if ple:
        p_ref, pg_ref, pwg_ref, pwp_ref = refs[:4]
        refs = refs[4:]
    o_ref, acc_ref = refs

    x = x_ref[...]
    start = 0
    for m_ref in mix_refs:
        width = m_ref.shape[1]
        x = x + _dot(m_ref[...].astype(BF16), wmix_ref[start:start + width, :])
        start += width
    xn = (x * _rms_scale(x) * g_ref[...]).astype(BF16)
    for c in range(n_chunks):
        cols = slice(c * fc, (c + 1) * fc)
        gate = _dot(xn, wg_ref[:, cols])
        up = _dot(xn, wu_ref[:, cols])
        act = (gate * _sigmoid(gate) * up).astype(BF16)
        part = _dot(act, wd_ref[cols, :])
        if c == 0:
            acc_ref[...] = part
        else:
            acc_ref[...] += part
    y = x + 0.5 * acc_ref[...]
    if ple:
        yn = (y * _rms_scale(y) * pg_ref[...]).astype(BF16)
        y = y + _sigmoid(_dot(yn, pwg_ref[...])) * _dot(p_ref[0].astype(BF16), pwp_ref[...])
    o_ref[...] = y


def _ffn(h, gain, wg, wu, wd, layer, half, tm, mix=None, ple=None):
    n, d = h.shape
    f = wg.shape[-1]
    fc = FF_CHUNK if f % FF_CHUNK == 0 else f
    const = lambda i: (0, 0)
    row = lambda i: (i, 0)
    pick = lambda i: (layer, half, 0, 0)
    whole = lambda a: pl.BlockSpec(a.shape, const, pipeline_mode=pl.Buffered(1))
    args = [h, gain.reshape(1, d), wg, wu, wd]
    in_specs = [
        pl.BlockSpec((tm, d), row),
        pl.BlockSpec((1, d), const),
        pl.BlockSpec((None, None, d, f), pick, pipeline_mode=pl.Buffered(1)),
        pl.BlockSpec((None, None, d, f), pick, pipeline_mode=pl.Buffered(1)),
        pl.BlockSpec((None, None, f, d), pick, pipeline_mode=pl.Buffered(1)),
    ]
    n_mix = 0
    if mix is not None:
        xs, w_mix = mix
        assert sum(x.shape[1] for x in xs) == w_mix.shape[0]
        n_mix = len(xs)
        args += list(xs) + [w_mix]
        in_specs += [pl.BlockSpec((tm, x.shape[1]), row) for x in xs] + [whole(w_mix)]
    if ple is not None:
        p, p_gain, w_gate, w_proj = ple
        args += [p, p_gain.reshape(1, d), w_gate, w_proj]
        in_specs += [pl.BlockSpec((1, tm, p.shape[2]), lambda i: (layer, i, 0)),
                     pl.BlockSpec((1, d), const), whole(w_gate), whole(w_proj)]
    return pl.pallas_call(
        functools.partial(_ffn_body, n_chunks=f // fc, fc=fc, n_mix=n_mix, ple=ple is not None),
        out_shape=jax.ShapeDtypeStruct((n, d), F32),
        grid=(n // tm,),
        in_specs=in_specs,
        out_specs=pl.BlockSpec((tm, d), row),
        scratch_shapes=[pltpu.VMEM((tm, d), F32)],
        compiler_params=_params(("parallel",)),
        name="ffn",
    )(*args)


def _norm_proj_body(x_ref, g_ref, w_ref, *o_refs, outs):
    x = x_ref[...]
    xn = (x * _rms_scale(x) * g_ref[...]).astype(BF16)
    start = 0
    for o_ref, (width, _, scale) in zip(o_refs, outs):
        y = _dot(xn, w_ref[:, start:start + width])
        o_ref[...] = (y if scale is None else y * scale).astype(o_ref.dtype)
        start += width


def _norm_proj(h, gain, w, outs, tm):
    n, d = h.shape
    widths = [o[0] for o in outs]
    assert sum(widths) == w.shape[1]
    return pl.pallas_call(
        functools.partial(_norm_proj_body, outs=outs),
        out_shape=[jax.ShapeDtypeStruct((n, width), dtype) for width, dtype, _ in outs],
        grid=(n // tm,),
        in_specs=[
            pl.BlockSpec((tm, d), lambda i: (i, 0)),
            pl.BlockSpec((1, d), lambda i: (0, 0)),
            pl.BlockSpec(w.shape, lambda i: (0, 0), pipeline_mode=pl.Buffered(1)),
        ],
        out_specs=[pl.BlockSpec((tm, width), lambda i: (i, 0)) for width in widths],
        compiler_params=_params(("parallel",)),
        name="norm_proj",
    )(h, gain.reshape(1, d), w)


def _sb_body(q_ref, k_ref, v_ref, o_ref, acc_ref, *, blk, qsub, n_pairs):
    qi = pl.program_id(2)
    per = LANES // HEAD_DIM
    assert per == 2
    tq = qsub * blk
    later = jnp.where(lax.broadcasted_iota(jnp.int32, (blk, blk), 0)
                      > lax.broadcasted_iota(jnp.int32, (blk, blk), 1), 1.0, 0.0).astype(BF16)
    later2 = jnp.concatenate([later, later], axis=0)
    first = lax.broadcasted_iota(jnp.int32, (tq, LANES), 1) < HEAD_DIM
    heads = [(p, h) for p in range(n_pairs) for h in range(per)]
    q = []
    for p, h in heads:
        qp = q_ref[0, :, p * LANES:(p + 1) * LANES]
        q.append(jnp.where(first if h == 0 else ~first, qp, jnp.zeros_like(qp)))

    def block(j, carry, diag):
        start = pl.multiple_of(j * blk, blk)
        rows = pl.ds(start, blk)
        r0 = 0 if diag is None else diag * blk
        causal = None
        if diag is not None:
            causal = (lax.broadcasted_iota(jnp.int32, (tq - r0, blk), 1)
                      < lax.broadcasted_iota(jnp.int32, (tq - r0, blk), 0))
        n = len(heads)
        z, split, rc, w, pv = [None] * n, [None] * n, [None] * n, [None] * n, [None] * n
        first_col = [None] * n

        def scores(i):
            p = heads[i][0]
            z[i] = _dot_nt(q[i][r0:], k_ref[0, rows, p * LANES:(p + 1) * LANES])

        def keep_terms(i):
            sp = jnp.maximum(z[i], 0.0) + jnp.log(1.0 + jnp.exp2(_neg_abs(z[i]))) * LOG2E
            z[i] = z[i] - sp
            if causal is not None:
                sp = jnp.where(causal, sp, 0.0)
            first_col[i] = sp[:, 0:1]
            hi = sp.astype(BF16)
            if SB_CUMSUM_TERMS == 1:
                split[i] = hi
            else:
                split[i] = jnp.concatenate([hi, (sp - hi.astype(F32)).astype(BF16)], axis=1)

        def cumulate(i):
            rc[i] = _dot(split[i], later if SB_CUMSUM_TERMS == 1 else later2)

        def weights(i):
            x = z[i] - rc[i]
            if carry is not None:
                x = x - carry[i]
            x = jnp.exp2(x)
            if causal is not None:
                x = jnp.where(causal, x, 0.0)
            w[i] = x.astype(BF16)

        def values(i):
            p = heads[i][0]
            pv[i] = _dot(w[i], v_ref[0, rows, p * LANES:(p + 1) * LANES])

        _emit_skewed(n, (scores, keep_terms, cumulate, weights, values))
        return pv, [r[:, 0:1] + f for r, f in zip(rc, first_col)]

    carry = None
    for diag in reversed(range(qsub)):
        r0 = diag * blk
        pv, tot = block(qi * qsub + diag, None if carry is None else [c[r0:] for c in carry], diag)
        if r0:
            tot = [jnp.concatenate([jnp.zeros((r0, 1), F32), t], axis=0) for t in tot]
        for i, x in enumerate(pv):
            if carry is None:
                if r0:
                    acc_ref[i, :r0, :] = jnp.zeros((r0, LANES), F32)
                acc_ref[i, r0:, :] = x
            else:
                acc_ref[i, r0:, :] += x
        carry = tot if carry is None else [c + t for c, t in zip(carry, tot)]

    def step(it, carry):
        pv, tot = block(qi * qsub - 1 - it, carry, None)
        for i, x in enumerate(pv):
            acc_ref[i] += x
        return tuple(c + t for c, t in zip(carry, tot))

    lax.fori_loop(0, qi * qsub, step, tuple(carry))
    for p in range(n_pairs):
        o_ref[0, :, p * LANES:(p + 1) * LANES] = jnp.where(first, acc_ref[per * p], acc_ref[per * p + 1])


def _sb_attention(q, kv, blk, qsub, n_pairs):
    b, t, _ = q.shape
    width = n_pairs * LANES
    groups = SB_W // width
    tq = qsub * blk
    return pl.pallas_call(
        functools.partial(_sb_body, blk=blk, qsub=qsub, n_pairs=n_pairs),
        out_shape=jax.ShapeDtypeStruct((b, t, SB_W), F32),
        grid=(b, groups, t // tq),
        in_specs=[
            pl.BlockSpec((1, tq, width), lambda bi, g, qi: (bi, qi, g)),
            pl.BlockSpec((1, t, width), lambda bi, g, qi: (bi, 0, g)),
            pl.BlockSpec((1, t, width), lambda bi, g, qi: (bi, 0, groups + g)),
        ],
        out_specs=pl.BlockSpec((1, tq, width), lambda bi, g, qi: (bi, qi, g)),
        scratch_shapes=[pltpu.VMEM((2 * n_pairs, tq, LANES), F32)],
        compiler_params=_params(("parallel", "parallel", "arbitrary")),
        name="sb_attention",
    )(q, kv, kv)


def _half_rms_scale(x, first):
    sq = x * x
    ms0 = jnp.sum(jnp.where(first, sq, 0.0), axis=-1, keepdims=True) * (1.0 / HEAD_DIM)
    ms1 = jnp.sum(jnp.where(first, 0.0, sq), axis=-1, keepdims=True) * (1.0 / HEAD_DIM)
    return jnp.where(first, lax.rsqrt(ms0 + EPS), lax.rsqrt(ms1 + EPS))


def _swa_body(sink_ref, q_ref, kvc_ref, kvp_ref, qg_ref, kg_ref, o_ref, *, n_blocks, slopes):
    tile = pl.program_id(1)
    w = WINDOW
    group = SWA_HEADS // SWA_KV_HEADS
    per = LANES // HEAD_DIM
    assert per == 2 and SWA_KVW == LANES
    first = lax.broadcasted_iota(jnp.int32, (1, LANES), 1) < HEAD_DIM

    def both_halves(x):
        swapped = pltpu.roll(x, HEAD_DIM, axis=1)
        return jnp.where(first, x, swapped), jnp.where(first, swapped, x)

    kv = jnp.concatenate([kvp_ref[0], kvc_ref[0]], axis=0)
    k = kv[:, :SWA_KVW]
    k = k * _half_rms_scale(k, first) * kg_ref[...]
    k_dup = [x.astype(BF16) for x in both_halves(k)]
    v_dup = [x.astype(BF16) for x in both_halves(kv[:, SWA_KVW:])]

    q_heads = []
    for g2 in range(SWA_QW // LANES):
        q = q_ref[0, :, g2 * LANES:(g2 + 1) * LANES]
        q = q * _half_rms_scale(q, first) * (qg_ref[...] * (HEAD_DIM ** -0.5 * LOG2E))
        q_heads.append(jnp.where(first, q, 0.0).astype(BF16))
        q_heads.append(jnp.where(first, 0.0, q).astype(BF16))

    qpos = lax.broadcasted_iota(jnp.int32, (w, 2 * w), 0)
    kpos = lax.broadcasted_iota(jnp.int32, (w, 2 * w), 1)
    dist = qpos + w - kpos
    band = (dist >= 0) & (dist < w)
    distf = dist.astype(F32)
    has_prev = (tile > 0) | (kpos >= w)
    bias, sink = [], []
    for kvh in range(SWA_KV_HEADS):
        heads = range(kvh * group, (kvh + 1) * group)
        bias.append(jnp.concatenate(
            [jnp.where(band, (-slopes[h] * LOG2E) * distf, -jnp.inf) for h in heads], axis=0))
        sink.append(jnp.concatenate(
            [jnp.full((w, 1), sink_ref[h] * LOG2E, F32) for h in heads], axis=0))

    units = [(j, kvh) for j in range(n_blocks) for kvh in range(SWA_KV_HEADS)]
    n = len(units)
    s, e, denom, res = [None] * n, [None] * n, [None] * n, [None] * n

    def scores(i):
        j, kvh = units[i]
        q4 = jnp.concatenate([q_heads[kvh * group + g][j * w:(j + 1) * w] for g in range(group)], axis=0)
        s[i] = _dot_nt(q4, k_dup[kvh][j * w:(j + 2) * w]) + bias[kvh]
        if j == 0:
            s[i] = jnp.where(jnp.concatenate([has_prev] * group, axis=0), s[i], -jnp.inf)

    def softmax(i):
        kvh = units[i][1]
        m = jnp.maximum(jnp.max(s[i], axis=-1, keepdims=True), sink[kvh])
        p = jnp.exp2(s[i] - m)
        denom[i] = jnp.sum(p, axis=-1, keepdims=True) + jnp.exp2(sink[kvh] - m)
        e[i] = p.astype(BF16)

    def values(i):
        j, kvh = units[i]
        res[i] = _dot(e[i], v_dup[kvh][j * w:(j + 2) * w]) / denom[i]

    def store(i):
        j, kvh = units[i]
        for g2 in range(group // per):
            piece = [res[i][(per * g2 + h) * w:(per * g2 + h + 1) * w] for h in range(per)]
            lanes = (kvh * (group // per) + g2) * LANES
            o_ref[0, j * w:(j + 1) * w, lanes:lanes + LANES] = jnp.where(first, piece[0], piece[1])

    _emit_skewed(n, (scores, softmax, values, store))


def _swa_attention(proj, q_gain, k_gain, sinks, n_blocks):
    b, t, _ = proj.shape
    w = WINDOW
    tq = n_blocks * w
    kv_blk = SWA_QW // (2 * SWA_KVW)
    slopes = tuple(float(2.0 ** (-8.0 * (i + 1) / SWA_HEADS)) for i in range(SWA_HEADS))
    gain2 = lambda g: jnp.concatenate([g, g]).reshape(1, LANES)
    return pl.pallas_call(
        functools.partial(_swa_body, n_blocks=n_blocks, slopes=slopes),
        out_shape=jax.ShapeDtypeStruct((b, t, SWA_QW), F32),
        grid=(b, t // tq),
        in_specs=[
            pl.BlockSpec(memory_space=pltpu.SMEM),
            pl.BlockSpec((1, tq, SWA_QW), lambda bi, i: (bi, i, 0)),
            pl.BlockSpec((1, tq, 2 * SWA_KVW), lambda bi, i: (bi, i, kv_blk)),
            pl.BlockSpec((1, w, 2 * SWA_KVW), lambda bi, i: (bi, jnp.maximum(i * n_blocks - 1, 0), kv_blk)),
            pl.BlockSpec((1, LANES), lambda bi, i: (0, 0)),
            pl.BlockSpec((1, LANES), lambda bi, i: (0, 0)),
        ],
        out_specs=pl.BlockSpec((1, tq, SWA_QW), lambda bi, i: (bi, i, 0)),
        compiler_params=_params(("parallel", "parallel")),
        name="swa_attention",
    )(sinks, proj, proj, proj, gain2(q_gain), gain2(k_gain))


def _gdn_in_body(x_ref, halo_ref, g_ref, w_ref, wba_ref, cw_ref, alog_ref, dtb_ref,
                 qkv_ref, z_ref, gb_ref, xn_ref, pre_ref, *, tm, cols, sub):
    ti = pl.program_id(1)
    step = pl.program_id(2)
    conv_steps = GDN_CONV_W // cols
    qk_steps = GDN_KW // cols
    nh = GDN_V_HEADS

    @pl.when(step == 0)
    def _():
        xh = halo_ref[0]
        x = x_ref[0]
        xn_ref[0:CONV_HALO, :] = (xh * _rms_scale(xh) * g_ref[...]).astype(BF16)
        xn_ref[CONV_HALO:, :] = (x * _rms_scale(x) * g_ref[...]).astype(BF16)
        ba = _dot(xn_ref[CONV_HALO:, :], wba_ref[...])
        beta = _sigmoid(ba[:, :nh])
        g = -jnp.exp(alog_ref[...]) * _softplus(ba[:, nh:2 * nh] + dtb_ref[...])
        r = lax.broadcasted_iota(jnp.int32, (GDN_CHUNK, GDN_CHUNK), 0)
        c = lax.broadcasted_iota(jnp.int32, (GDN_CHUNK, GDN_CHUNK), 1)
        lower = jnp.where(r >= c, 1.0, 0.0).astype(F32)
        for chunk in range(tm // GDN_CHUNK):
            rows = slice(chunk * GDN_CHUNK, (chunk + 1) * GDN_CHUNK)
            gb_ref[0, rows, :] = jnp.concatenate([_dot_f32(lower, g[rows]), beta[rows]], axis=-1)

    def conv_step(normalise):
        def project(i):
            pre = _dot(xn_ref[...], w_ref[:, i * sub:(i + 1) * sub])
            pre_ref[i, 0:CONV_HALO, :] = jnp.where(ti > 0, pre[:CONV_HALO], 0.0)
            pre_ref[i, CONV_HALO:, :] = pre[CONV_HALO:]

        def conv(i):
            ext = pre_ref[i]
            y = None
            for tap in range(GDN_CONV):
                back = GDN_CONV - 1 - tap
                rows = ext if back == 0 else pltpu.roll(ext, back, axis=0)
                term = rows[CONV_HALO:] * cw_ref[tap:tap + 1, i * sub:(i + 1) * sub]
                y = term if y is None else y + term
            y = y / (1.0 + jnp.exp2(y * (-LOG2E)))
            if not normalise:
                qkv_ref[0, :, i * sub:(i + 1) * sub] = y
                return
            q_scale = jnp.where(step < qk_steps, GDN_HEAD_DIM ** -0.5, 1.0)
            for head in range(sub // GDN_HEAD_DIM):
                lanes = slice(head * GDN_HEAD_DIM, (head + 1) * GDN_HEAD_DIM)
                yh = y[:, lanes]
                inv = lax.rsqrt(jnp.sum(yh * yh, axis=-1, keepdims=True) + EPS) * q_scale
                qkv_ref[0, :, i * sub + head * GDN_HEAD_DIM:i * sub + (head + 1) * GDN_HEAD_DIM] = yh * inv

        _emit_skewed(cols // sub, (project, conv))

    pl.when(step < 2 * qk_steps)(functools.partial(conv_step, True))
    pl.when((step >= 2 * qk_steps) & (step < conv_steps))(functools.partial(conv_step, False))

    @pl.when(step >= conv_steps)
    def _():
        z_ref[0] = _dot(xn_ref[CONV_HALO:, :], w_ref[...])


def _gdn_in(h3, gain, w, conv_w, a_log, dt_bias, tm):
    b, t, d = h3.shape
    nh = GDN_V_HEADS
    cols = GDN_IN_COLS
    conv_steps = GDN_CONV_W // cols
    steps = (GDN_CONV_W + GDN_VW) // cols
    w_ba = w[:, GDN_CONV_W + GDN_VW:]
    return pl.pallas_call(
        functools.partial(_gdn_in_body, tm=tm, cols=cols, sub=GDN_IN_SUB),
        out_shape=[
            jax.ShapeDtypeStruct((b, t, GDN_CONV_W), F32),
            jax.ShapeDtypeStruct((b, t, GDN_VW), F32),
            jax.ShapeDtypeStruct((b, t, 2 * nh), F32),
        ],
        grid=(b, t // tm, steps),
        in_specs=[
            pl.BlockSpec((1, tm, d), lambda bi, ti, s: (bi, ti, 0)),
            pl.BlockSpec((1, CONV_HALO, d),
                         lambda bi, ti, s: (bi, jnp.maximum(ti * (tm // CONV_HALO) - 1, 0), 0)),
            pl.BlockSpec((1, d), lambda bi, ti, s: (0, 0)),
            pl.BlockSpec((d, cols), lambda bi, ti, s: (0, s)),
            pl.BlockSpec(w_ba.shape, lambda bi, ti, s: (0, 0)),
            pl.BlockSpec((GDN_CONV, cols), lambda bi, ti, s: (0, jnp.minimum(s, conv_steps - 1))),
            pl.BlockSpec((1, nh), lambda bi, ti, s: (0, 0)),
            pl.BlockSpec((1, nh), lambda bi, ti, s: (0, 0)),
        ],
        out_specs=[
            pl.BlockSpec((1, tm, cols), lambda bi, ti, s: (bi, ti, jnp.minimum(s, conv_steps - 1))),
            pl.BlockSpec((1, tm, cols), lambda bi, ti, s: (bi, ti, jnp.maximum(s - conv_steps, 0))),
            pl.BlockSpec((1, tm, 2 * nh), lambda bi, ti, s: (bi, ti, 0)),
        ],
        scratch_shapes=[
            pltpu.VMEM((CONV_HALO + tm, d), BF16),
            pltpu.VMEM((cols // GDN_IN_SUB, CONV_HALO + tm, GDN_IN_SUB), F32),
        ],
        compiler_params=_params(("parallel", "parallel", "arbitrary")),
        name="gdn_in",
    )(h3, h3, gain.reshape(1, d), w, w_ba, conv_w, a_log.reshape(1, nh), dt_bias.reshape(1, nh))


def _split(x):
    hi = x.astype(BF16)
    return hi, (x - hi.astype(F32)).astype(BF16)


def _block_diag(y, mask):
    return jnp.where(mask, jnp.concatenate([y, y], axis=0), jnp.zeros((), y.dtype))


def _mm3(x, y_hi, y_lo):
    x_hi, x_lo = _split(x)
    return _dot(jnp.concatenate([x_hi, x_hi, x_lo], axis=1), jnp.concatenate([y_hi, y_lo, y_hi], axis=0))


def _pair_mm(x, y, mask):
    y_hi, y_lo = _split(y)
    return _mm3(x, _block_diag(y_hi, mask), _block_diag(y_lo, mask))


def _unit_lower_inverse_pairs(a_list, eye, same_block, bd_mask):
    mm = lambda xs, ys: [_pair_mm(x, y, bd_mask) for x, y in zip(xs, ys)]
    add = lambda xs, ys: [x + y for x, y in zip(xs, ys)]
    c = a_list[0].shape[0]
    ad = [jnp.where(same_block, a, 0.0) for a in a_list]
    lo = [a - d for a, d in zip(a_list, ad)]
    dinv = [eye - d for d in ad]
    power = ad
    span = 2
    while span < INV_BLOCK:
        power = mm(power, power)
        dinv = add(dinv, mm(dinv, power))
        span *= 2
    n = mm(dinv, lo)
    m = [eye - x for x in n]
    power = n
    span = 2
    while span < c // INV_BLOCK:
        power = mm(power, power)
        m = add(m, mm(m, power))
        span *= 2
    return mm(m, dinv)


def _gdn_local_body(q_ref, k_ref, v_ref, gb_ref, gr_ref, u_ref, wq_ref, kt_ref, at_ref, *, n_chunks):
    pair = pl.program_id(1)
    cs = GDN_CHUNK
    nh = GDN_V_HEADS
    d = GDN_HEAD_DIM
    rep = GDN_V_HEADS // GDN_K_HEADS
    assert rep == 2 and rep * cs == LANES

    ii = lax.broadcasted_iota(jnp.int32, (cs, rep * cs), 0)
    lane = lax.broadcasted_iota(jnp.int32, (cs, rep * cs), 1)
    jj = lane % cs
    first = lane < cs
    eye = jnp.where(ii == jj, 1.0, 0.0).astype(F32)
    same_block = (ii // INV_BLOCK) == (jj // INV_BLOCK)
    r2 = lax.broadcasted_iota(jnp.int32, (rep * cs, rep * cs), 0)
    c2 = lax.broadcasted_iota(jnp.int32, (rep * cs, rep * cs), 1)
    bd_mask = (r2 // cs) == (c2 // cs)
    lane_gb = lax.broadcasted_iota(jnp.int32, (cs, 2 * nh), 1)
    zeros_rhs = jnp.zeros((cs, 2 * d), BF16)
    chunks = range(n_chunks)
    rows = [slice(c * cs, (c + 1) * cs) for c in chunks]

    def column(gb, idx):
        return jnp.sum(jnp.where(lane_gb == idx, gb, 0.0), axis=-1, keepdims=True)

    q = [q_ref[0, r, :] for r in rows]
    k = [k_ref[0, r, :] for r in rows]
    gb = [gb_ref[0, r, :] for r in rows]
    gc = [[column(g, rep * pair + h) for h in range(rep)] for g in gb]
    beta = [[column(g, nh + rep * pair + h) for h in range(rep)] for g in gb]
    gc_row = [gr_ref[0, 0, c] for c in chunks]
    decay = [jnp.exp(jnp.where(ii >= jj, jnp.where(first, gc[c][0], gc[c][1]) - gc_row[c], -jnp.inf))
             for c in chunks]
    k16 = [x.astype(BF16) for x in k]
    kk16 = [jnp.concatenate([x, x], axis=0) for x in k16]
    a = [jnp.where(ii > jj, _dot_nt(k16[c], kk16[c]) * jnp.where(first, beta[c][0], beta[c][1]) * decay[c], 0.0)
         for c in chunks]
    for c in chunks:
        at_ref[0, 0, rows[c], :] = jnp.where(
            ii >= jj, _dot_nt(q[c].astype(BF16), kk16[c]) * decay[c], 0.0).astype(BF16)
    tinv = _unit_lower_inverse_pairs(a, eye, same_block, bd_mask)

    eg = [[jnp.exp(gc[c][h]) for h in range(rep)] for c in chunks]
    sol = []
    for c in chunks:
        rhs_hi, rhs_lo = [], []
        for h in range(rep):
            v = v_ref[0, rows[c], h * d:(h + 1) * d]
            hi, lo = _split(jnp.concatenate([v * beta[c][h], k[c] * (beta[c][h] * eg[c][h])], axis=-1))
            pad = [zeros_rhs] * rep
            rhs_hi.append(jnp.concatenate(pad[:h] + [hi] + pad[h + 1:], axis=1))
            rhs_lo.append(jnp.concatenate(pad[:h] + [lo] + pad[h + 1:], axis=1))
        rhs_hi = jnp.concatenate(rhs_hi, axis=0)
        rhs_lo = jnp.concatenate(rhs_lo, axis=0)
        sol.append(_mm3(tinv[c], rhs_hi, rhs_lo))

    for c in chunks:
        for h in range(rep):
            u_ref[0, rows[c], h * d:(h + 1) * d] = sol[c][:, 2 * d * h:2 * d * h + d]
            wq_ref[0, h, c, :cs, :] = sol[c][:, 2 * d * h + d:2 * d * (h + 1)].astype(BF16)
            wq_ref[0, h, c, cs:, :] = (q[c] * eg[c][h]).astype(BF16)
            g_last = gc_row[c][:, h * cs + cs - 1:(h + 1) * cs]
            kt_ref[0, h, c] = (k[c] * jnp.exp(g_last - gc[c][h])).T.astype(BF16)


def _gdn_local(qkv, gb, gc_rows, n_chunks):
    b, t, _ = qkv.shape
    nh = GDN_V_HEADS
    rep = GDN_V_HEADS // GDN_K_HEADS
    cs = GDN_CHUNK
    tt = n_chunks * cs
    d = GDN_HEAD_DIM
    return pl.pallas_call(
        functools.partial(_gdn_local_body, n_chunks=n_chunks),
        out_shape=[
            jax.ShapeDtypeStruct((b, t, GDN_VW), F32),
            jax.ShapeDtypeStruct((b, nh, t // cs, 2 * cs, d), BF16),
            jax.ShapeDtypeStruct((b, nh, t // cs, d, cs), BF16),
            jax.ShapeDtypeStruct((b, GDN_K_HEADS, t, rep * cs), BF16),
        ],
        grid=(b, GDN_K_HEADS, t // tt),
        in_specs=[
            pl.BlockSpec((1, tt, d), lambda bi, p, ti: (bi, ti, p)),
            pl.BlockSpec((1, tt, d), lambda bi, p, ti: (bi, ti, GDN_K_HEADS + p)),
            pl.BlockSpec((1, tt, rep * d), lambda bi, p, ti: (bi, ti, GDN_K_HEADS + p)),
            pl.BlockSpec((1, tt, 2 * nh), lambda bi, p, ti: (bi, ti, 0)),
            pl.BlockSpec((1, 1, n_chunks, 1, rep * cs), lambda bi, p, ti: (bi, p, ti, 0, 0)),
        ],
        out_specs=[
            pl.BlockSpec((1, tt, rep * d), lambda bi, p, ti: (bi, ti, p)),
            pl.BlockSpec((1, rep, n_chunks, 2 * cs, d), lambda bi, p, ti: (bi, p, ti, 0, 0)),
            pl.BlockSpec((1, rep, n_chunks, d, cs), lambda bi, p, ti: (bi, p, ti, 0, 0)),
            pl.BlockSpec((1, 1, tt, rep * cs), lambda bi, p, ti: (bi, p, ti, 0)),
        ],
        compiler_params=_params(("parallel", "parallel", "parallel")),
        name="gdn_local",
    )(qkv, qkv, qkv, gb, gc_rows)


def _gdn_scan_body(u_ref, wq_ref, kt_ref, at_ref, gr_ref, z_ref, g_ref, y_ref, s_ref, *, n_chunks, n_pairs):
    cs = GDN_CHUNK
    d = GDN_HEAD_DIM
    rep = GDN_V_HEADS // GDN_K_HEADS
    heads = range(rep * n_pairs)

    @pl.when(pl.program_id(2) == 0)
    def _():
        s_ref[...] = jnp.zeros_like(s_ref)

    zeros_v = jnp.zeros((cs, d), BF16)
    states = [s_ref[i] for i in heads]
    for c in range(n_chunks):
        rows = slice(c * cs, (c + 1) * cs)
        s16 = [s.astype(BF16) for s in states]
        both = [_dot(wq_ref[0, i, c], s16[i]) for i in heads]
        v_new = [(u_ref[0, rows, i * d:(i + 1) * d] - both[i][:cs]).astype(BF16) for i in heads]
        for p in range(n_pairs):
            gc_row = gr_ref[0, p, c]
            diag = []
            for h in range(rep):
                pad = [zeros_v] * rep
                diag.append(jnp.concatenate(pad[:h] + [v_new[rep * p + h]] + pad[h + 1:], axis=1))
            o_attn = _dot(at_ref[0, p, rows, :], jnp.concatenate(diag, axis=0))
            for h in range(rep):
                i = rep * p + h
                o = both[i][cs:] + o_attn[:, h * d:(h + 1) * d]
                z = z_ref[0, rows, i * d:(i + 1) * d]
                gate = z / (1.0 + jnp.exp2(z * (-LOG2E)))
                y_ref[0, rows, i * d:(i + 1) * d] = (o * _rms_scale(o) * g_ref[...] * gate).astype(BF16)
                g_last = gc_row[:, h * cs + cs - 1:(h + 1) * cs]
                states[i] = states[i] * jnp.exp(g_last) + _dot(kt_ref[0, i, c], v_new[i])
    for i in heads:
        s_ref[i] = states[i]


def _gdn_scan(u, wq, kt, attn, gc_rows, z, gain, n_chunks, n_pairs):
    b, t, _ = u.shape
    rep = GDN_V_HEADS // GDN_K_HEADS
    cs = GDN_CHUNK
    tt = n_chunks * cs
    d = GDN_HEAD_DIM
    nhs = rep * n_pairs
    return pl.pallas_call(
        functools.partial(_gdn_scan_body, n_chunks=n_chunks, n_pairs=n_pairs),
        out_shape=jax.ShapeDtypeStruct((b, t, GDN_VW), BF16),
        grid=(b, GDN_K_HEADS // n_pairs, t // tt),
        in_specs=[
            pl.BlockSpec((1, tt, nhs * d), lambda bi, g, ti: (bi, ti, g)),
            pl.BlockSpec((1, nhs, n_chunks, 2 * cs, d), lambda bi, g, ti: (bi, g, ti, 0, 0)),
            pl.BlockSpec((1, nhs, n_chunks, d, cs), lambda bi, g, ti: (bi, g, ti, 0, 0)),
            pl.BlockSpec((1, n_pairs, tt, rep * cs), lambda bi, g, ti: (bi, g, ti, 0)),
            pl.BlockSpec((1, n_pairs, n_chunks, 1, rep * cs), lambda bi, g, ti: (bi, g, ti, 0, 0)),
            pl.BlockSpec((1, tt, nhs * d), lambda bi, g, ti: (bi, ti, g)),
            pl.BlockSpec((1, d), lambda bi, g, ti: (0, 0)),
        ],
        out_specs=pl.BlockSpec((1, tt, nhs * d), lambda bi, g, ti: (bi, ti, g)),
        scratch_shapes=[pltpu.VMEM((nhs, d, d), F32)],
        compiler_params=_params(("parallel", "parallel", "arbitrary")),
        name="gdn_scan",
    )(u, wq, kt, attn, gc_rows, z, gain.reshape(1, d))


def _attention_mixer(h, b, t, gain, w_in, q_gain, k_gain, sinks, w_out, tm):
    d = h.shape[1]
    swa_w = SWA_QW + 2 * SWA_KVW
    outs = ((SB_W, BF16, HEAD_DIM ** -0.5 * LOG2E), (2 * SB_W, BF16, None), (swa_w, F32, None))
    sb_q, sb_kv, swa = _norm_proj(h, gain, w_in.astype(BF16), outs, tm)
    a_out = _sb_attention(sb_q.reshape(b, t, SB_W), sb_kv.reshape(b, t, 2 * SB_W),
                          min(SB_BLOCK, t), SB_Q_BLOCKS if t >= SB_Q_BLOCKS * SB_BLOCK else 1,
                          SB_PAIRS).reshape(b * t, SB_W)
    b_out = _swa_attention(swa.reshape(b, t, swa_w), q_gain, k_gain, sinks,
                           min(SWA_TILE_BLOCKS, t // WINDOW)).reshape(b * t, SWA_QW)
    return [a_out, b_out], w_out.astype(BF16)


def _gdn_mixer(h, b, t, gain, w_in, conv_w, a_log, dt_bias, out_gain, w_out, tm):
    nh = GDN_V_HEADS
    ba_w = w_in.shape[1] - GDN_CONV_W - GDN_VW
    assert ba_w == 2 * nh
    qkv, z, gb = _gdn_in(h.reshape(b, t, -1), gain, w_in.astype(BF16), conv_w, a_log, dt_bias,
                         min(GDN_IN_TILE, t))
    rep = GDN_V_HEADS // GDN_K_HEADS
    gc_rows = gb[:, :, :nh].reshape(b, t // GDN_CHUNK, GDN_CHUNK, GDN_K_HEADS, rep)
    gc_rows = jnp.transpose(gc_rows, (0, 3, 1, 4, 2)).reshape(
        b, GDN_K_HEADS, t // GDN_CHUNK, 1, rep * GDN_CHUNK)
    u, wq, kt, attn = _gdn_local(qkv, gb, gc_rows, min(GDN_LOCAL_CHUNKS, t // GDN_CHUNK))
    y = _gdn_scan(u, wq, kt, attn, gc_rows, z, out_gain, min(GDN_SCAN_CHUNKS, t // GDN_CHUNK), GDN_SCAN_PAIRS)
    return [y.reshape(b * t, GDN_VW)], w_out.astype(BF16)


def kernel(x, p, ffn_norm, ffn_w_gate, ffn_w_up, ffn_w_down, mix_norm, att_w_in, att_q_norm, att_k_norm, att_sinks, att_w_out, gdn_w_in, gdn_conv_w, gdn_a_log, gdn_dt_bias, gdn_out_norm, gdn_w_out, ple_norm, ple_w_gate, ple_w_proj):
    b, t, d = x.shape
    n = b * t
    tm = min(TOKEN_TILE, n)
    depth = p.shape[0]
    h = x.reshape(n, d)
    wg, wu, wd = ffn_w_gate.astype(BF16), ffn_w_up.astype(BF16), ffn_w_down.astype(BF16)
    for i in range(depth):
        j = i // 2
        h = _ffn(h, ffn_norm[i, 0], wg, wu, wd, i, 0, tm)
        if i % 2 == 0:
            mix = _attention_mixer(h, b, t, mix_norm[i], att_w_in[j], att_q_norm[j], att_k_norm[j],
                                   att_sinks[j], att_w_out[j], tm)
        else:
            mix = _gdn_mixer(h, b, t, mix_norm[i], gdn_w_in[j], gdn_conv_w[j], gdn_a_log[j],
                             gdn_dt_bias[j], gdn_out_norm[j], gdn_w_out[j], tm)
        ple = (p.reshape(depth, n, -1), ple_norm[i], ple_w_gate[i].astype(BF16), ple_w_proj[i].astype(BF16))
        h = _ffn(h, ffn_norm[i, 1], wg, wu, wd, i, 1, tm, mix=mix, ple=ple)
    return h.reshape(b, t, d)
```
